```python
import jax, jax.numpy as jnp
from jax import lax
import numpy as np

D_MODEL = 2048
BATCH = 4
SEQ = 4096
DEPTH = 1

D_INNER = D_MODEL
GLA_WIDTH = D_INNER // 2
GLA_HEADS = 4
GLA_HEAD_V = GLA_WIDTH // GLA_HEADS
GLA_HEAD_K = GLA_HEAD_V // 2
GLA_KEY_WIDTH = GLA_HEADS * GLA_HEAD_K
GLA_LOWRANK = 16
GLA_TAU = 16.0
HGRN_WIDTH = D_INNER - GLA_WIDTH
HGRN_EXPAND = 128
HGRN_HEADS = HGRN_WIDTH // HGRN_EXPAND
HGRN_HEAD_V = HGRN_WIDTH // HGRN_HEADS
CHUNK = 64
EPS = 1e-6

SPLIT_SIZES = (GLA_KEY_WIDTH, GLA_KEY_WIDTH, GLA_WIDTH, GLA_WIDTH, GLA_LOWRANK,
               HGRN_WIDTH, HGRN_WIDTH, HGRN_WIDTH, HGRN_WIDTH)
D_PROJ = sum(SPLIT_SIZES)

kernel_name = "hymba_gla_hgrn2_parallel_layer"


def rmsnorm(x, g):
    x32 = x.astype(jnp.float32)
    y = x32 * lax.rsqrt(jnp.mean(x32 * x32, axis=-1, keepdims=True) + EPS)
    return (y * g.astype(jnp.float32)).astype(x.dtype)


def head_rmsnorm(o, g):
    B, T, H, dv = o.shape
    o32 = o.astype(jnp.float32)
    y = o32 * lax.rsqrt(jnp.mean(o32 * o32, axis=-1, keepdims=True) + EPS)
    return y.reshape(B, T, H * dv) * g.astype(jnp.float32)


def split_heads(a, n_heads):
    B, T, D = a.shape
    return a.reshape(B, T, n_heads, D // n_heads)


def chunked_gated_linear_attention(q, k, v, log_g, scale):
    B, T, H, dk = q.shape
    dv = v.shape[-1]
    n_chunks = T // CHUNK

    def to_chunks(a):
        a = a.astype(jnp.float32)
        return a.reshape(B, n_chunks, CHUNK, H, a.shape[-1]).transpose(1, 0, 3, 2, 4)

    qc = to_chunks(q) * scale
    kc, vc, gc = to_chunks(k), to_chunks(v), to_chunks(log_g)
    causal = jnp.tril(jnp.ones((CHUNK, CHUNK), dtype=bool))[:, :, None]

    def step(S, inp):
        q_, k_, v_, g_ = inp
        b = jnp.cumsum(g_, axis=-2)
        diff = b[..., :, None, :] - b[..., None, :, :]
        decay = jnp.exp(jnp.where(causal, diff, -jnp.inf))
        A = jnp.einsum('bhid,bhjd,bhijd->bhij', q_, k_, decay)
        o_intra = jnp.einsum('bhij,bhje->bhie', A, v_)
        o_inter = jnp.einsum('bhid,bhde->bhie', q_ * jnp.exp(b), S)
        b_last = b[..., -1:, :]
        k_dec = k_ * jnp.exp(b_last - b)
        S_new = jnp.exp(b_last)[..., 0, :, None] * S + jnp.einsum('bhjd,bhje->bhde', k_dec, v_)
        return S_new, o_intra + o_inter

    S0 = jnp.zeros((B, H, dk, dv), jnp.float32)
    _, out = lax.scan(step, S0, (qc, kc, vc, gc))
    return out.transpose(1, 0, 3, 2, 4).reshape(B, T, H, dv)


def setup_inputs(seed: int = 0) -> dict:
    key = jax.random.key(seed)
    ks = jax.random.split(key, 12)
    f32 = jnp.float32
    x = jax.random.normal(ks[0], (BATCH, SEQ, D_MODEL), f32)
    norm_w = 1.0 + 0.02 * jax.random.normal(ks[1], (DEPTH, D_MODEL), f32)
    w_in = jax.random.normal(ks[2], (DEPTH, D_MODEL, D_PROJ), f32) * D_MODEL ** -0.5
    gla_w_up = jax.random.normal(ks[3], (DEPTH, GLA_LOWRANK, GLA_KEY_WIDTH), f32) * GLA_LOWRANK ** -0.5
    gla_b_up = 0.1 * jax.random.normal(ks[4], (DEPTH, GLA_KEY_WIDTH), f32)
    gla_norm_w = 1.0 + 0.02 * jax.random.normal(ks[5], (DEPTH, GLA_WIDTH), f32)
    hgrn_lb_logits = 0.5 * jax.random.normal(ks[6], (DEPTH + 1, HGRN_WIDTH), f32)
    hgrn_norm_w = 1.0 + 0.02 * jax.random.normal(ks[7], (DEPTH, HGRN_WIDTH), f32)
    w_out = jax.random.normal(ks[8], (DEPTH, D_INNER, D_MODEL), f32) * D_INNER ** -0.5
    final_norm_w = 1.0 + 0.02 * jax.random.normal(ks[9], (D_MODEL,), f32)
    return {"x": x, "norm_w": norm_w, "w_in": w_in, "gla_w_up": gla_w_up,
            "gla_b_up": gla_b_up, "gla_norm_w": gla_norm_w,
            "hgrn_lb_logits": hgrn_lb_logits, "hgrn_norm_w": hgrn_norm_w,
            "w_out": w_out, "final_norm_w": final_norm_w}


def reference(x, norm_w, w_in, gla_w_up, gla_b_up, gla_norm_w, hgrn_lb_logits,
              hgrn_norm_w, w_out, final_norm_w):
    offsets = [int(o) for o in np.cumsum(SPLIT_SIZES)[:-1]]
    lb_all = jnp.cumsum(jax.nn.softmax(hgrn_lb_logits.astype(jnp.float32), axis=0), axis=0)
    for l in range(DEPTH):
        h = rmsnorm(x, norm_w[l])
        proj = h @ w_in[l]
        gq, gk, gv, gg, glr, hq, hf, hi, hg = jnp.split(proj, offsets, axis=-1)

        log_a = jax.nn.log_sigmoid((glr @ gla_w_up[l] + gla_b_up[l]).astype(jnp.float32)) / GLA_TAU
        o_a = chunked_gated_linear_attention(
            split_heads(gq, GLA_HEADS), split_heads(gk, GLA_HEADS),
            split_heads(gv, GLA_HEADS), split_heads(log_a, GLA_HEADS),
            GLA_HEAD_K ** -0.5)
        o_a = head_rmsnorm(o_a, gla_norm_w[l]) * jax.nn.silu(gg.astype(jnp.float32))

        lb = lb_all[l]
        f = lb + (1.0 - lb) * jax.nn.sigmoid(hf.astype(jnp.float32))
        o_b = chunked_gated_linear_attention(
            split_heads(hq, HGRN_HEADS), split_heads(1.0 - f, HGRN_HEADS),
            split_heads(hi, HGRN_HEADS), split_heads(jnp.log(f), HGRN_HEADS),
            HGRN_EXPAND ** -0.5)
        o_b = head_rmsnorm(o_b, hgrn_norm_w[l]) * jax.nn.silu(hg.astype(jnp.float32))

        mixed = jnp.concatenate([o_a, o_b], axis=-1).astype(x.dtype)
        x = x + mixed @ w_out[l]
    return rmsnorm(x, final_norm_w)
```

```python
import functools

import numpy as np
import jax
import jax.numpy as jnp
from jax import lax
from jax.experimental import pallas as pl
from jax.experimental.pallas import tpu as pltpu

D_MODEL = 2048
GLA_HEADS = 4
GLA_HEAD_K = 128
GLA_HEAD_V = 256
GLA_KEY_WIDTH = GLA_HEADS * GLA_HEAD_K
GLA_WIDTH = GLA_HEADS * GLA_HEAD_V
GLA_LOWRANK = 16
GLA_TAU = 16.0
HGRN_HEADS = 8
HGRN_HEAD_K = 128
HGRN_HEAD_V = 128
HGRN_WIDTH = HGRN_HEADS * HGRN_HEAD_V
EPS = 1e-6

LANE = 128
BLK = 128
LEVELS = (64, 32, 16, 8, 4, 2, 1)
N_PLANES = len(LEVELS) + 2
DIAG_LEVEL = len(LEVELS)
VMEM_LIMIT = 56 * 1024 * 1024

GLA_Q0, GLA_K0, GLA_V0, GLA_G0 = 0, 4, 8, 16
HG_Q0, HG_F0, HG_I0, HG_G0 = 24, 32, 40, 48
N_COLBLK = 56


def _plane_matrix():
    r = np.arange(BLK)[:, None]
    t = np.arange(BLK)[None, :]
    planes = []
    for s in LEVELS:
        start = (r // (2 * s)) * (2 * s)
        mid = start + s - 1
        upper = (r - start) >= s
        planes.append(np.where(upper, (t > mid) & (t <= r), (t > r) & (t <= mid)))
    planes.append(t <= r)
    planes.append(t > r)
    return np.concatenate(planes, axis=0).astype(np.float32)


def _level_matrix():
    i = np.arange(BLK)[:, None]
    j = np.arange(BLK)[None, :]
    x = i ^ j
    msb = np.zeros_like(x)
    for bit in range(7):
        msb = np.where(x >> bit > 0, bit, msb)
    lvl = (len(LEVELS) - 1) - msb
    lvl = np.where(i == j, DIAG_LEVEL, lvl)
    lvl = np.where(i < j, -1, lvl)
    return lvl.astype(np.int32)


def _proj_kernel(x_ref, nw_ref, w_ref, wlr_ref, out_ref, lr_ref, h_ref):
    j = pl.program_id(1)

    @pl.when(j == 0)
    def _():
        x = x_ref[...]
        ms = jnp.mean(x * x, axis=-1, keepdims=True)
        h = (x * lax.rsqrt(ms + EPS) * nw_ref[...]).astype(jnp.bfloat16)
        h_ref[...] = h
        lr_ref[...] = jnp.dot(h, wlr_ref[...], preferred_element_type=jnp.float32).astype(lr_ref.dtype)

    acc = jnp.dot(h_ref[...], w_ref[...], preferred_element_type=jnp.float32)
    for c in range(out_ref.shape[0]):
        out_ref[c] = acc[:, c * LANE:(c + 1) * LANE].astype(out_ref.dtype)


def _input_projection(x2, norm_w, w_main, w_lr, tm=1024, tn=512):
    bt = x2.shape[0]
    n = w_main.shape[1]
    nb = tn // LANE
    return pl.pallas_call(
        _proj_kernel,
        grid=(bt // tm, n // tn),
        in_specs=[
            pl.BlockSpec((tm, D_MODEL), lambda i, j: (i, 0)),
            pl.BlockSpec((1, D_MODEL), lambda i, j: (0, 0)),
            pl.BlockSpec((D_MODEL, tn), lambda i, j: (0, j)),
            pl.BlockSpec((D_MODEL, LANE), lambda i, j: (0, 0)),
        ],
        out_specs=[
            pl.BlockSpec((nb, tm, LANE), lambda i, j: (j, i, 0)),
            pl.BlockSpec((tm, LANE), lambda i, j: (i, 0)),
        ],
        out_shape=[
            jax.ShapeDtypeStruct((n // LANE, bt, LANE), jnp.bfloat16),
            jax.ShapeDtypeStruct((bt, LANE), jnp.bfloat16),
        ],
        scratch_shapes=[pltpu.VMEM((tm, D_MODEL), jnp.bfloat16)],
        compiler_params=pltpu.CompilerParams(
            dimension_semantics=("arbitrary", "arbitrary"),
            vmem_limit_bytes=VMEM_LIMIT),
        name="rmsnorm_in_proj",
    )(x2, norm_w, w_main, w_lr)


def _gla_block(q, k, logg_planes, v_bf, st_ref, hh, lvl, row):
    dcol = jnp.sum(q * k, axis=-1, keepdims=True)
    a = jnp.where(lvl == DIAG_LEVEL, dcol, 0.0)
    for li, s in enumerate(LEVELS):
        upper = (row & s) != 0
        p = (jnp.where(upper, q, k) * jnp.exp(logg_planes[li])).astype(jnp.bfloat16)
        g = lax.dot_general(p, p, (((1,), (1,)), ((), ())), preferred_element_type=jnp.float32)
        a = jnp.where(lvl == li, g, a)
    e_b = logg_planes[len(LEVELS)]
    e_dec = logg_planes[len(LEVELS) + 1]
    q_in = (q * jnp.exp(e_b)).astype(jnp.bfloat16)
    k_dec = (k * jnp.exp(e_dec)).astype(jnp.bfloat16)
    st = st_ref[hh]
    o = jnp.dot(a.astype(jnp.bfloat16), v_bf, preferred_element_type=jnp.float32)
    o = o + lax.dot_general(q_in, st.astype(jnp.bfloat16), (((1,), (1,)), ((), ())),
                            preferred_element_type=jnp.float32)
    decay = jnp.exp(e_b[BLK - 1:BLK, :])
    upd = lax.dot_general(v_bf, k_dec, (((0,), (0,)), ((), ())), preferred_element_type=jnp.float32)
    st_ref[hh] = st * decay + upd
    return o


def _head_out(o, gate, gain):
    ms = jnp.mean(o * o, axis=-1, keepdims=True)
    y = o * lax.rsqrt(ms + EPS) * gain
    return y * (gate / (1.0 + jnp.exp(-gate)))


def _gla_pair_kernel(q_ref, k_ref, v_ref, g_ref, lr_ref, wup_ref, bup_ref, gain_ref, mall_ref, lvl_ref,
                     out_ref, st_ref):
    @pl.when(pl.program_id(2) == 0)
    def _():
        st_ref[...] = jnp.zeros_like(st_ref)

    nsub = out_ref.shape[0] // BLK
    scale = GLA_HEAD_K ** -0.5

    def body(c, carry):
        rows = pl.ds(pl.multiple_of(c * BLK, BLK), BLK)
        lvl = lvl_ref[...]
        row = lax.broadcasted_iota(jnp.int32, (BLK, LANE), 0)
        z = jnp.dot(lr_ref[rows, :], wup_ref[...], preferred_element_type=jnp.float32) + bup_ref[...]
        logg = (jnp.minimum(z, 0.0) - jnp.log(1.0 + jnp.exp(-jnp.abs(z)))) * (1.0 / GLA_TAU)
        planes = jnp.dot(mall_ref[...], logg.astype(jnp.bfloat16), preferred_element_type=jnp.float32)
        for hh in range(2):
            q = q_ref[hh, rows, :].astype(jnp.float32) * scale
            k = k_ref[hh, rows, :].astype(jnp.float32)
            pls = [planes[p * BLK:(p + 1) * BLK, hh * LANE:(hh + 1) * LANE] for p in range(N_PLANES)]
            v_bf = jnp.concatenate([v_ref[2 * hh, rows, :], v_ref[2 * hh + 1, rows, :]], axis=-1)
            o = _gla_block(q, k, pls, v_bf, st_ref, hh, lvl, row)
            gate = jnp.concatenate([g_ref[2 * hh, rows, :], g_ref[2 * hh + 1, rows, :]],
                                   axis=-1).astype(jnp.float32)
            gain = gain_ref[:, hh * GLA_HEAD_V:(hh + 1) * GLA_HEAD_V]
            out_ref[rows, hh * GLA_HEAD_V:(hh + 1) * GLA_HEAD_V] = _head_out(o, gate, gain).astype(out_ref.dtype)
        return carry

    lax.fori_loop(0, nsub, body, 0)


def _hgrn_pair_kernel(q_ref, f_ref, i_ref, g_ref, lb_ref, gain_ref, mall_ref, lvl_ref, out_ref, st_ref):
    @pl.when(pl.program_id(2) == 0)
    def _():
        st_ref[...] = jnp.zeros_like(st_ref)

    nsub = out_ref.shape[0] // BLK
    scale = HGRN_HEAD_K ** -0.5
    lg = lb_ref[...]
    e = jnp.exp(lg - jnp.max(lg, axis=0, keepdims=True))
    lb = e[0:1, :] / jnp.sum(e, axis=0, keepdims=True)

    def body(c, carry):
        rows = pl.ds(pl.multiple_of(c * BLK, BLK), BLK)
        lvl = lvl_ref[...]
        row = lax.broadcasted_iota(jnp.int32, (BLK, LANE), 0)
        hf = jnp.concatenate([f_ref[0, rows, :], f_ref[1, rows, :]], axis=-1).astype(jnp.float32)
        f = lb + (1.0 - lb) * (1.0 / (1.0 + jnp.exp(-hf)))
        logg = jnp.log(f)
        kk = 1.0 - f
        planes = jnp.dot(mall_ref[...], logg.astype(jnp.bfloat16), preferred_element_type=jnp.float32)
        for hh in range(2):
            q = q_ref[hh, rows, :].astype(jnp.float32) * scale
            k = kk[:, hh * LANE:(hh + 1) * LANE]
            pls = [planes[p * BLK:(p + 1) * BLK, hh * LANE:(hh + 1) * LANE] for p in range(N_PLANES)]
            o = _gla_block(q, k, pls, i_ref[hh, rows, :], st_ref, hh, lvl, row)
            gate = g_ref[hh, rows, :].astype(jnp.float32)
            gain = gain_ref[:, hh * HGRN_HEAD_V:(hh + 1) * HGRN_HEAD_V]
            out_ref[rows, hh * HGRN_HEAD_V:(hh + 1) * HGRN_HEAD_V] = _head_out(o, gate, gain).astype(out_ref.dtype)
        return carry

    lax.fori_loop(0, nsub, body, 0)


def _mixer_call(kernel, name, proj3, col_specs, extra, extra_specs, n_pairs, dv, batch, seq, tb):
    nt = seq // tb
    mall = jnp.asarray(_plane_matrix(), dtype=jnp.bfloat16)
    lvl = jnp.asarray(_level_matrix())
    const = lambda shape: pl.BlockSpec(shape, lambda b, p, t: (0,) * len(shape))
    in_specs = [pl.BlockSpec((nblk, tb, LANE), functools.partial(
        lambda b, p, t, nblk, first: (first // nblk + p, b * nt + t, 0), nblk=nblk, first=first))
        for nblk, first in col_specs]
    return pl.pallas_call(
        kernel,
        grid=(batch, n_pairs, nt),
        in_specs=in_specs + extra_specs + [const(mall.shape), const(lvl.shape)],
        out_specs=pl.BlockSpec((tb, 2 * dv), lambda b, p, t: (b * nt + t, p)),
        out_shape=jax.ShapeDtypeStruct((batch * seq, n_pairs * 2 * dv), jnp.bfloat16),
        scratch_shapes=[pltpu.VMEM((2, dv, LANE), jnp.float32)],
        compiler_params=pltpu.CompilerParams(
            dimension_semantics=("arbitrary", "arbitrary", "arbitrary"),
            vmem_limit_bytes=VMEM_LIMIT),
        name=name,
    )(*([proj3] * len(col_specs)), *extra, mall, lvl)


def _out_kernel(ma_ref, mb_ref, wa_ref, wb_ref, x_ref, fw_ref, out_ref):
    y = jnp.dot(ma_ref[...], wa_ref[...], preferred_element_type=jnp.float32)
    y = y + jnp.dot(mb_ref[...], wb_ref[...], preferred_element_type=jnp.float32)
    y = y + x_ref[...]
    ms = jnp.mean(y * y, axis=-1, keepdims=True)
    out_ref[...] = y * lax.rsqrt(ms + EPS) * fw_ref[...]


def _output_projection(ma, mb, wa, wb, x2, final_w, tm=512):
    bt = x2.shape[0]
    return pl.pallas_call(
        _out_kernel,
        grid=(bt // tm,),
        in_specs=[
            pl.BlockSpec((tm, GLA_WIDTH), lambda i: (i, 0)),
            pl.BlockSpec((tm, HGRN_WIDTH), lambda i: (i, 0)),
            pl.BlockSpec((GLA_WIDTH, D_MODEL), lambda i: (0, 0)),
            pl.BlockSpec((HGRN_WIDTH, D_MODEL), lambda i: (0, 0)),
            pl.BlockSpec((tm, D_MODEL), lambda i: (i, 0)),
            pl.BlockSpec((1, D_MODEL), lambda i: (0, 0)),
        ],
        out_specs=pl.BlockSpec((tm, D_MODEL), lambda i: (i, 0)),
        out_shape=jax.ShapeDtypeStruct((bt, D_MODEL), jnp.float32),
        compiler_params=pltpu.CompilerParams(
            dimension_semantics=("arbitrary",),
            vmem_limit_bytes=VMEM_LIMIT),
        name="out_proj_residual_norm",
    )(ma, mb, wa, wb, x2, final_w)


def kernel(x, norm_w, w_in, gla_w_up, gla_b_up, gla_norm_w, hgrn_lb_logits, hgrn_norm_w, w_out, final_norm_w):
    batch, seq, d = x.shape
    assert d == D_MODEL and norm_w.shape[0] == 1 and seq % 1024 == 0
    x2 = x.reshape(batch * seq, d)
    w = w_in[0]
    lr0 = GLA_G0 * LANE + GLA_WIDTH
    w_main = jnp.concatenate([w[:, :lr0], w[:, lr0 + GLA_LOWRANK:]], axis=1).astype(jnp.bfloat16)
    w_lr = jnp.pad(w[:, lr0:lr0 + GLA_LOWRANK], ((0, 0), (0, LANE - GLA_LOWRANK))).astype(jnp.bfloat16)

    proj3, lr = _input_projection(x2, norm_w, w_main, w_lr)

    tb = 1024
    nt = seq // tb
    wup = jnp.pad(gla_w_up[0], ((0, LANE - GLA_LOWRANK), (0, 0))).astype(jnp.bfloat16)
    mixed_a = _mixer_call(
        _gla_pair_kernel, "gla_pair", proj3,
        [(2, GLA_Q0), (2, GLA_K0), (4, GLA_V0), (4, GLA_G0)],
        [lr, wup, gla_b_up, gla_norm_w],
        [pl.BlockSpec((tb, LANE), lambda b, p, t: (b * nt + t, 0)),
         pl.BlockSpec((LANE, 2 * LANE), lambda b, p, t: (0, p)),
         pl.BlockSpec((1, 2 * LANE), lambda b, p, t: (0, p)),
         pl.BlockSpec((1, 2 * GLA_HEAD_V), lambda b, p, t: (0, p))],
        GLA_HEADS // 2, GLA_HEAD_V, batch, seq, tb)
    mixed_b = _mixer_call(
        _hgrn_pair_kernel, "hgrn_pair", proj3,
        [(2, HG_Q0), (2, HG_F0), (2, HG_I0), (2, HG_G0)],
        [hgrn_lb_logits, hgrn_norm_w],
        [pl.BlockSpec((hgrn_lb_logits.shape[0], 2 * LANE), lambda b, p, t: (0, p)),
         pl.BlockSpec((1, 2 * HGRN_HEAD_V), lambda b, p, t: (0, p))],
        HGRN_HEADS // 2, HGRN_HEAD_V, batch, seq, tb)

    wo = w_out[0].astype(jnp.bfloat16)
    out = _output_projection(mixed_a, mixed_b, wo[:GLA_WIDTH], wo[GLA_WIDTH:], x2,
                             final_norm_w.reshape(1, d))
    return out.reshape(batch, seq, d)
```

```python
import functools

import numpy as np
import jax
import jax.numpy as jnp
from jax import lax
from jax.experimental import pallas as pl
from jax.experimental.pallas import tpu as pltpu

D_MODEL = 2048
GLA_HEADS = 4
GLA_HEAD_K = 128
GLA_HEAD_V = 256
GLA_KEY_WIDTH = GLA_HEADS * GLA_HEAD_K
GLA_WIDTH = GLA_HEADS * GLA_HEAD_V
GLA_LOWRANK = 16
GLA_TAU = 16.0
HGRN_HEADS = 8
HGRN_HEAD_K = 128
HGRN_HEAD_V = 128
HGRN_WIDTH = HGRN_HEADS * HGRN_HEAD_V
EPS = 1e-6

LANE = 128
SUBLANE = 8
BLK = 128
LEVELS = (64, 32, 16, 8, 4, 2, 1)
N_PLANES = len(LEVELS) + 2
DIAG_LEVEL = len(LEVELS)
VMEM_LIMIT = 56 * 1024 * 1024
LOG2E = 1.4426950408889634

GLA_Q0, GLA_K0, GLA_V0, GLA_G0 = 0, 4, 8, 16
HG_Q0, HG_F0, HG_I0, HG_G0 = 24, 32, 40, 48
N_COLBLK = 56


def _plane_matrix():
    r = np.arange(BLK)[:, None]
    t = np.arange(BLK)[None, :]
    planes = []
    for s in LEVELS:
        start = (r // (2 * s)) * (2 * s)
        mid = start + s - 1
        upper = (r - start) >= s
        planes.append(np.where(upper, (t > mid) & (t <= r), (t > r) & (t <= mid)))
    planes.append(t <= r)
    planes.append(t > r)
    return np.concatenate(planes, axis=0).astype(np.float32)


def _level_matrix():
    i = np.arange(BLK)[:, None]
    j = np.arange(BLK)[None, :]
    x = i ^ j
    msb = np.zeros_like(x)
    for bit in range(7):
        msb = np.where(x >> bit > 0, bit, msb)
    lvl = (len(LEVELS) - 1) - msb
    lvl = np.where(i == j, DIAG_LEVEL, lvl)
    lvl = np.where(i < j, -1, lvl)
    return lvl.astype(np.int32)


def _proj_kernel(x_ref, nw_ref, w_ref, wlr_ref, out_ref, lr_ref, h_ref):
    j = pl.program_id(1)

    @pl.when(j == 0)
    def _():
        x = x_ref[...]
        ms = jnp.mean(x * x, axis=-1, keepdims=True)
        h = (x * lax.rsqrt(ms + EPS) * nw_ref[...]).astype(jnp.bfloat16)
        h_ref[...] = h
        lr_ref[...] = jnp.dot(h, wlr_ref[...], preferred_element_type=jnp.float32).astype(lr_ref.dtype)

    acc = jnp.dot(h_ref[...], w_ref[...], preferred_element_type=jnp.float32)
    for c in range(out_ref.shape[0]):
        out_ref[c] = acc[:, c * LANE:(c + 1) * LANE].astype(out_ref.dtype)


def _input_projection(x2, norm_w, w_main, w_lr, tm=1024, tn=512):
    bt = x2.shape[0]
    n = w_main.shape[1]
    nb = tn // LANE
    return pl.pallas_call(
        _proj_kernel,
        grid=(bt // tm, n // tn),
        in_specs=[
            pl.BlockSpec((tm, D_MODEL), lambda i, j: (i, 0)),
            pl.BlockSpec((1, D_MODEL), lambda i, j: (0, 0)),
            pl.BlockSpec((D_MODEL, tn), lambda i, j: (0, j)),
            pl.BlockSpec((D_MODEL, LANE), lambda i, j: (0, 0)),
        ],
        out_specs=[
            pl.BlockSpec((nb, tm, LANE), lambda i, j: (j, i, 0)),
            pl.BlockSpec((tm, LANE), lambda i, j: (i, 0)),
        ],
        out_shape=[
            jax.ShapeDtypeStruct((n // LANE, bt, LANE), jnp.bfloat16),
            jax.ShapeDtypeStruct((bt, LANE), jnp.bfloat16),
        ],
        scratch_shapes=[pltpu.VMEM((tm, D_MODEL), jnp.bfloat16)],
        compiler_params=pltpu.CompilerParams(
            dimension_semantics=("arbitrary", "arbitrary"),
            vmem_limit_bytes=VMEM_LIMIT),
        name="rmsnorm_in_proj",
    )(x2, norm_w, w_main, w_lr)


def _upper_runs(x, s):
    return jnp.concatenate([x[r:r + s] for r in range(s, BLK, 2 * s)], axis=0)


def _prep_block(q, k, planes, hh):
    lanes = slice(hh * LANE, (hh + 1) * LANE)
    row = lax.broadcasted_iota(jnp.int32, (BLK, LANE), 0)
    lhs, w = [], []
    for li, s in enumerate(LEVELS):
        f = jnp.exp2(planes[li * BLK:(li + 1) * BLK, lanes])
        if s >= SUBLANE:
            qk = jnp.concatenate([(q if (r & s) else k)[r:r + SUBLANE] for r in range(0, BLK, SUBLANE)], axis=0)
            p32 = qk * f
            lhs.append(_upper_runs(p32, s).astype(jnp.bfloat16))
        else:
            p32 = jnp.where((row & s) != 0, q, k) * f
            lhs.append(None)
        w.append(p32.astype(jnp.bfloat16))
    e_b = planes[len(LEVELS) * BLK:(len(LEVELS) + 1) * BLK, lanes]
    e_dec = planes[(len(LEVELS) + 1) * BLK:(len(LEVELS) + 2) * BLK, lanes]
    return dict(
        lhs=lhs, w=w,
        q_in=(q * jnp.exp2(e_b)).astype(jnp.bfloat16),
        k_dec=(k * jnp.exp2(e_dec)).astype(jnp.bfloat16),
        decay=jnp.exp2(e_b[BLK - 1:BLK, :]),
        dcol=jnp.sum(q * k, axis=-1, keepdims=True))


def _level_scores(ops):
    out = [[], []]
    for li in range(len(LEVELS)):
        w2 = jnp.concatenate([ops[0]['w'][li], ops[1]['w'][li]], axis=1)
        wt = lax.bitcast_convert_type(lax.bitcast_convert_type(w2, jnp.uint16).T, jnp.bfloat16)
        for hh in range(2):
            lhs = ops[hh]['lhs'][li]
            lhs = ops[hh]['w'][li] if lhs is None else lhs
            out[hh].append(jnp.dot(lhs, wt[hh * LANE:(hh + 1) * LANE], preferred_element_type=jnp.float32))
    return out


def _assemble_scores(g, dcol, lvl_ref):
    groups = []
    for r in range(0, BLK, SUBLANE):
        lvl = lvl_ref[r:r + SUBLANE, :]
        a = jnp.where(lvl == DIAG_LEVEL, dcol[r:r + SUBLANE], 0.0)
        for li, s in enumerate(LEVELS):
            if s >= SUBLANE:
                if r & s:
                    k = (r // (2 * s)) * s + (r % s)
                    a = jnp.where(lvl == li, g[li][k:k + SUBLANE], a)
            else:
                a = jnp.where(lvl == li, g[li][r:r + SUBLANE], a)
        groups.append(a)
    return jnp.concatenate(groups, axis=0).astype(jnp.bfloat16)


def _finish_block(a, ops, v_bf, st_ref, hh):
    st = st_ref[hh]
    o = jnp.dot(a, v_bf, preferred_element_type=jnp.float32)
    o = o + jnp.dot(ops['q_in'], st.astype(jnp.bfloat16).T, preferred_element_type=jnp.float32)
    upd = jnp.dot(v_bf.T, ops['k_dec'], preferred_element_type=jnp.float32)
    st_ref[hh] = st * ops['decay'] + upd
    return o


def _head_out(o, gate, gain):
    ms = jnp.mean(o * o, axis=-1, keepdims=True)
    y = o * lax.rsqrt(ms + EPS) * gain
    return y * (gate / (1.0 + jnp.exp(-gate)))


def _pair_pipeline(nblk, gates_fn, q_fn, v_fn, out_fn, mall_ref, lvl_ref, st_ref):
    def exponents(c):
        logg, ks = gates_fn(c)
        return jnp.dot(mall_ref[...], logg.astype(jnp.bfloat16), preferred_element_type=jnp.float32), ks

    def operands(c, planes_ks):
        planes, ks = planes_ks
        return [_prep_block(q_fn(c, hh), ks[hh], planes, hh) for hh in range(2)]

    planes = {0: exponents(0)}
    if nblk > 1:
        planes[1] = exponents(1)
    ops = {0: operands(0, planes.pop(0))}
    for c in range(nblk):
        g = _level_scores(ops[c])
        if c + 2 < nblk:
            planes[c + 2] = exponents(c + 2)
        a = [_assemble_scores(g[hh], ops[c][hh]['dcol'], lvl_ref) for hh in range(2)]
        if c + 1 < nblk:
            ops[c + 1] = operands(c + 1, planes.pop(c + 1))
        cur = ops.pop(c)
        for hh in range(2):
            out_fn(c, hh, _finish_block(a[hh], cur[hh], v_fn(c, hh), st_ref, hh))


def _gla_pair_kernel(q_ref, k_ref, v_ref, g_ref, lr_ref, wup_ref, bup_ref, gain_ref, mall_ref, lvl_ref,
                     out_ref, st_ref):
    @pl.when(pl.program_id(2) == 0)
    def _():
        st_ref[...] = jnp.zeros_like(st_ref)

    scale = GLA_HEAD_K ** -0.5
    rows = lambda c: slice(c * BLK, (c + 1) * BLK)

    def gates_fn(c):
        z = jnp.dot(lr_ref[rows(c), :], wup_ref[...], preferred_element_type=jnp.float32) + bup_ref[...]
        logg = (jnp.minimum(z, 0.0) - jnp.log(1.0 + jnp.exp(-jnp.abs(z)))) * (LOG2E / GLA_TAU)
        return logg, [k_ref[hh, rows(c), :].astype(jnp.float32) for hh in range(2)]

    def q_fn(c, hh):
        return q_ref[hh, rows(c), :].astype(jnp.float32) * scale

    def v_fn(c, hh):
        return jnp.concatenate([v_ref[2 * hh, rows(c), :], v_ref[2 * hh + 1, rows(c), :]], axis=-1)

    def out_fn(c, hh, o):
        gate = jnp.concatenate([g_ref[2 * hh, rows(c), :], g_ref[2 * hh + 1, rows(c), :]],
                               axis=-1).astype(jnp.float32)
        cols = slice(hh * GLA_HEAD_V, (hh + 1) * GLA_HEAD_V)
        out_ref[rows(c), cols] = _head_out(o, gate, gain_ref[:, cols]).astype(out_ref.dtype)

    _pair_pipeline(out_ref.shape[0] // BLK, gates_fn, q_fn, v_fn, out_fn, mall_ref, lvl_ref, st_ref)


def _hgrn_pair_kernel(q_ref, f_ref, i_ref, g_ref, lb_ref, gain_ref, mall_ref, lvl_ref, out_ref, st_ref):
    @pl.when(pl.program_id(2) == 0)
    def _():
        st_ref[...] = jnp.zeros_like(st_ref)

    scale = HGRN_HEAD_K ** -0.5
    rows = lambda c: slice(c * BLK, (c + 1) * BLK)
    lg = lb_ref[...]
    e = jnp.exp(lg - jnp.max(lg, axis=0, keepdims=True))
    lb = e[0:1, :] / jnp.sum(e, axis=0, keepdims=True)

    def gates_fn(c):
        hf = jnp.concatenate([f_ref[0, rows(c), :], f_ref[1, rows(c), :]], axis=-1).astype(jnp.float32)
        f = lb + (1.0 - lb) * (1.0 / (1.0 + jnp.exp(-hf)))
        kk = 1.0 - f
        return jnp.log(f) * LOG2E, [kk[:, hh * LANE:(hh + 1) * LANE] for hh in range(2)]

    def q_fn(c, hh):
        return q_ref[hh, rows(c), :].astype(jnp.float32) * scale

    def v_fn(c, hh):
        return i_ref[hh, rows(c), :]

    def out_fn(c, hh, o):
        gate = g_ref[hh, rows(c), :].astype(jnp.float32)
        cols = slice(hh * HGRN_HEAD_V, (hh + 1) * HGRN_HEAD_V)
        out_ref[rows(c), cols] = _head_out(o, gate, gain_ref[:, cols]).astype(out_ref.dtype)

    _pair_pipeline(out_ref.shape[0] // BLK, gates_fn, q_fn, v_fn, out_fn, mall_ref, lvl_ref, st_ref)


def _mixer_call(kernel, name, proj3, col_specs, extra, extra_specs, n_pairs, dv, batch, seq, tb):
    nt = seq // tb
    mall = jnp.asarray(_plane_matrix(), dtype=jnp.bfloat16)
    lvl = jnp.asarray(_level_matrix())
    const = lambda shape: pl.BlockSpec(shape, lambda b, p, t: (0,) * len(shape))
    in_specs = [pl.BlockSpec((nblk, tb, LANE), functools.partial(
        lambda b, p, t, nblk, first: (first // nblk + p, b * nt + t, 0), nblk=nblk, first=first))
        for nblk, first in col_specs]
    return pl.pallas_call(
        kernel,
        grid=(batch, n_pairs, nt),
        in_specs=in_specs + extra_specs + [const(mall.shape), const(lvl.shape)],
        out_specs=pl.BlockSpec((tb, 2 * dv), lambda b, p, t: (b * nt + t, p)),
        out_shape=jax.ShapeDtypeStruct((batch * seq, n_pairs * 2 * dv), jnp.bfloat16),
        scratch_shapes=[pltpu.VMEM((2, dv, LANE), jnp.float32)],
        compiler_params=pltpu.CompilerParams(
            dimension_semantics=("arbitrary", "arbitrary", "arbitrary"),
            vmem_limit_bytes=VMEM_LIMIT),
        name=name,
    )(*([proj3] * len(col_specs)), *extra, mall, lvl)


def _out_kernel(ma_ref, mb_ref, wa_ref, wb_ref, x_ref, fw_ref, out_ref):
    y = jnp.dot(ma_ref[...], wa_ref[...], preferred_element_type=jnp.float32)
    y = y + jnp.dot(mb_ref[...], wb_ref[...], preferred_element_type=jnp.float32)
    y = y + x_ref[...]
    ms = jnp.mean(y * y, axis=-1, keepdims=True)
    out_ref[...] = y * lax.rsqrt(ms + EPS) * fw_ref[...]


def _output_projection(ma, mb, wa, wb, x2, final_w, tm=512):
    bt = x2.shape[0]
    return pl.pallas_call(
        _out_kernel,
        grid=(bt // tm,),
        in_specs=[
            pl.BlockSpec((tm, GLA_WIDTH), lambda i: (i, 0)),
            pl.BlockSpec((tm, HGRN_WIDTH), lambda i: (i, 0)),
            pl.BlockSpec((GLA_WIDTH, D_MODEL), lambda i: (0, 0)),
            pl.BlockSpec((HGRN_WIDTH, D_MODEL), lambda i: (0, 0)),
            pl.BlockSpec((tm, D_MODEL), lambda i: (i, 0)),
            pl.BlockSpec((1, D_MODEL), lambda i: (0, 0)),
        ],
        out_specs=pl.BlockSpec((tm, D_MODEL), lambda i: (i, 0)),
        out_shape=jax.ShapeDtypeStruct((bt, D_MODEL), jnp.float32),
        compiler_params=pltpu.CompilerParams(
            dimension_semantics=("arbitrary",),
            vmem_limit_bytes=VMEM_LIMIT),
        name="out_proj_residual_norm",
    )(ma, mb, wa, wb, x2, final_w)


def kernel(x, norm_w, w_in, gla_w_up, gla_b_up, gla_norm_w, hgrn_lb_logits, hgrn_norm_w, w_out, final_norm_w):
    batch, seq, d = x.shape
    assert d == D_MODEL and norm_w.shape[0] == 1 and seq % 1024 == 0
    x2 = x.reshape(batch * seq, d)
    w = w_in[0]
    lr0 = GLA_G0 * LANE + GLA_WIDTH
    w_main = jnp.concatenate([w[:, :lr0], w[:, lr0 + GLA_LOWRANK:]], axis=1).astype(jnp.bfloat16)
    w_lr = jnp.pad(w[:, lr0:lr0 + GLA_LOWRANK], ((0, 0), (0, LANE - GLA_LOWRANK))).astype(jnp.bfloat16)

    proj3, lr = _input_projection(x2, norm_w, w_main, w_lr)

    tb = 1024
    nt = seq // tb
    wup = jnp.pad(gla_w_up[0], ((0, LANE - GLA_LOWRANK), (0, 0))).astype(jnp.bfloat16)
    mixed_a = _mixer_call(
        _gla_pair_kernel, "gla_pair", proj3,
        [(2, GLA_Q0), (2, GLA_K0), (4, GLA_V0), (4, GLA_G0)],
        [lr, wup, gla_b_up, gla_norm_w],
        [pl.BlockSpec((tb, LANE), lambda b, p, t: (b * nt + t, 0)),
         pl.BlockSpec((LANE, 2 * LANE), lambda b, p, t: (0, p)),
         pl.BlockSpec((1, 2 * LANE), lambda b, p, t: (0, p)),
         pl.BlockSpec((1, 2 * GLA_HEAD_V), lambda b, p, t: (0, p))],
        GLA_HEADS // 2, GLA_HEAD_V, batch, seq, tb)
    mixed_b = _mixer_call(
        _hgrn_pair_kernel, "hgrn_pair", proj3,
        [(2, HG_Q0), (2, HG_F0), (2, HG_I0), (2, HG_G0)],
        [hgrn_lb_logits, hgrn_norm_w],
        [pl.BlockSpec((hgrn_lb_logits.shape[0], 2 * LANE), lambda b, p, t: (0, p)),
         pl.BlockSpec((1, 2 * HGRN_HEAD_V), lambda b, p, t: (0, p))],
        HGRN_HEADS // 2, HGRN_HEAD_V, batch, seq, tb)

    wo = w_out[0].astype(jnp.bfloat16)
    out = _output_projection(mixed_a, mixed_b, wo[:GLA_WIDTH], wo[GLA_WIDTH:], x2,
                             final_norm_w.reshape(1, d))
    return out.reshape(batch, seq, d)
```

```python
import functools

import numpy as np
import jax
import jax.numpy as jnp
from jax import lax
from jax.experimental import pallas as pl
from jax.experimental.pallas import tpu as pltpu

D_MODEL = 2048
GLA_HEADS = 4
GLA_HEAD_K = 128
GLA_HEAD_V = 256
GLA_KEY_WIDTH = GLA_HEADS * GLA_HEAD_K
GLA_WIDTH = GLA_HEADS * GLA_HEAD_V
GLA_LOWRANK = 16
GLA_TAU = 16.0
HGRN_HEADS = 8
HGRN_HEAD_K = 128
HGRN_HEAD_V = 128
HGRN_WIDTH = HGRN_HEADS * HGRN_HEAD_V
EPS = 1e-6

LANE = 128
SUBLANE = 8
BLK = 128
LEVELS = (64, 32, 16, 8, 4, 2, 1)
N_PLANES = len(LEVELS) + 2
DIAG_LEVEL = len(LEVELS)
VMEM_LIMIT = 56 * 1024 * 1024
LOG2E = 1.4426950408889634

GLA_Q0, GLA_K0, GLA_V0, GLA_G0 = 0, 4, 8, 16
HG_Q0, HG_F0, HG_I0, HG_G0 = 24, 32, 40, 48
LR_BLK = 56
PROJ_CHUNK = 512


def _plane_matrix():
    r = np.arange(BLK)[:, None]
    t = np.arange(BLK)[None, :]
    planes = []
    for s in LEVELS:
        start = (r // (2 * s)) * (2 * s)
        mid = start + s - 1
        upper = (r - start) >= s
        planes.append(np.where(upper, (t > mid) & (t <= r), (t > r) & (t <= mid)))
    planes.append(t <= r)
    planes.append(t > r)
    return np.concatenate(planes, axis=0).astype(np.float32)


def _level_matrix():
    i = np.arange(BLK)[:, None]
    j = np.arange(BLK)[None, :]
    x = i ^ j
    msb = np.zeros_like(x)
    for bit in range(7):
        msb = np.where(x >> bit > 0, bit, msb)
    lvl = (len(LEVELS) - 1) - msb
    lvl = np.where(i == j, DIAG_LEVEL, lvl)
    lvl = np.where(i < j, -1, lvl)
    return lvl.astype(np.int32)


def _proj_kernel(x_ref, nw_ref, w_ref, out_ref):
    x = x_ref[...]
    ms = jnp.mean(x * x, axis=-1, keepdims=True)
    h = (x * lax.rsqrt(ms + EPS) * nw_ref[...]).astype(jnp.bfloat16)
    n = w_ref.shape[1]
    for j0 in range(0, n, PROJ_CHUNK):
        j1 = min(j0 + PROJ_CHUNK, n)
        acc = jnp.dot(h, w_ref[:, j0:j1], preferred_element_type=jnp.float32)
        for c in range((j1 - j0) // LANE):
            out_ref[j0 // LANE + c] = acc[:, c * LANE:(c + 1) * LANE].astype(out_ref.dtype)


def _input_projection(x2, norm_w, w_all, tm=256):
    bt = x2.shape[0]
    n = w_all.shape[1]
    return pl.pallas_call(
        _proj_kernel,
        grid=(bt // tm,),
        in_specs=[
            pl.BlockSpec((tm, D_MODEL), lambda i: (i, 0)),
            pl.BlockSpec((1, D_MODEL), lambda i: (0, 0)),
            pl.BlockSpec((D_MODEL, n), lambda i: (0, 0), pipeline_mode=pl.Buffered(1)),
        ],
        out_specs=pl.BlockSpec((n // LANE, tm, LANE), lambda i: (0, i, 0)),
        out_shape=jax.ShapeDtypeStruct((n // LANE, bt, LANE), jnp.bfloat16),
        compiler_params=pltpu.CompilerParams(
            dimension_semantics=("arbitrary",),
            vmem_limit_bytes=VMEM_LIMIT),
        name="rmsnorm_in_proj",
    )(x2, norm_w, w_all)


def _upper_runs(x, s):
    return jnp.concatenate([x[r:r + s] for r in range(s, BLK, 2 * s)], axis=0)


def _prep_block(q, k, planes, hh):
    lanes = slice(hh * LANE, (hh + 1) * LANE)
    row = lax.broadcasted_iota(jnp.int32, (BLK, LANE), 0)
    lhs, w = [], []
    for li, s in enumerate(LEVELS):
        f = jnp.exp2(planes[li * BLK:(li + 1) * BLK, lanes])
        if s >= SUBLANE:
            qk = jnp.concatenate([(q if (r & s) else k)[r:r + SUBLANE] for r in range(0, BLK, SUBLANE)], axis=0)
            p32 = qk * f
            lhs.append(_upper_runs(p32, s).astype(jnp.bfloat16))
        else:
            p32 = jnp.where((row & s) != 0, q, k) * f
            lhs.append(None)
        w.append(p32.astype(jnp.bfloat16))
    e_b = planes[len(LEVELS) * BLK:(len(LEVELS) + 1) * BLK, lanes]
    e_dec = planes[(len(LEVELS) + 1) * BLK:(len(LEVELS) + 2) * BLK, lanes]
    return dict(
        lhs=lhs, w=w,
        q_in=(q * jnp.exp2(e_b)).astype(jnp.bfloat16),
        k_dec=(k * jnp.exp2(e_dec)).astype(jnp.bfloat16),
        decay=jnp.exp2(e_b[BLK - 1:BLK, :]),
        dcol=jnp.sum(q * k, axis=-1, keepdims=True))


def _level_scores(ops):
    out = [[], []]
    for li in range(len(LEVELS)):
        w2 = jnp.concatenate([ops[0]['w'][li], ops[1]['w'][li]], axis=1)
        wt = lax.bitcast_convert_type(lax.bitcast_convert_type(w2, jnp.uint16).T, jnp.bfloat16)
        for hh in range(2):
            lhs = ops[hh]['lhs'][li]
            lhs = ops[hh]['w'][li] if lhs is None else lhs
            out[hh].append(jnp.dot(lhs, wt[hh * LANE:(hh + 1) * LANE], preferred_element_type=jnp.float32))
    return out


def _assemble_scores(g, dcol, lvl_ref):
    groups = []
    for r in range(0, BLK, SUBLANE):
        lvl = lvl_ref[r:r + SUBLANE, :]
        a = jnp.where(lvl == DIAG_LEVEL, dcol[r:r + SUBLANE], 0.0)
        for li, s in enumerate(LEVELS):
            if s >= SUBLANE:
                if r & s:
                    k = (r // (2 * s)) * s + (r % s)
                    a = jnp.where(lvl == li, g[li][k:k + SUBLANE], a)
            else:
                a = jnp.where(lvl == li, g[li][r:r + SUBLANE], a)
        groups.append(a)
    return jnp.concatenate(groups, axis=0).astype(jnp.bfloat16)


def _finish_block(a, ops, v_bf, st_ref, hh):
    st = st_ref[hh]
    o = jnp.dot(a, v_bf, preferred_element_type=jnp.float32)
    o = o + jnp.dot(ops['q_in'], st.astype(jnp.bfloat16).T, preferred_element_type=jnp.float32)
    upd = jnp.dot(v_bf.T, ops['k_dec'], preferred_element_type=jnp.float32)
    st_ref[hh] = st * ops['decay'] + upd
    return o


def _head_out(o, gate, gain):
    ms = jnp.mean(o * o, axis=-1, keepdims=True)
    y = o * lax.rsqrt(ms + EPS) * gain
    return y * (gate / (1.0 + jnp.exp(-gate)))


def _pair_pipeline(nblk, gates_fn, q_fn, v_fn, out_fn, mall_ref, lvl_ref, st_ref):
    def exponents(c):
        logg, ks = gates_fn(c)
        return jnp.dot(mall_ref[...], logg.astype(jnp.bfloat16), preferred_element_type=jnp.float32), ks

    def operands(c, planes_ks):
        planes, ks = planes_ks
        return [_prep_block(q_fn(c, hh), ks[hh], planes, hh) for hh in range(2)]

    planes = {0: exponents(0)}
    if nblk > 1:
        planes[1] = exponents(1)
    ops = {0: operands(0, planes.pop(0))}
    for c in range(nblk):
        g = _level_scores(ops[c])
        if c + 2 < nblk:
            planes[c + 2] = exponents(c + 2)
        a = [_assemble_scores(g[hh], ops[c][hh]['dcol'], lvl_ref) for hh in range(2)]
        if c + 1 < nblk:
            ops[c + 1] = operands(c + 1, planes.pop(c + 1))
        cur = ops.pop(c)
        for hh in range(2):
            out_fn(c, hh, _finish_block(a[hh], cur[hh], v_fn(c, hh), st_ref, hh))


def _gla_pair_kernel(q_ref, k_ref, v_ref, g_ref, lr_ref, wup_ref, bup_ref, gain_ref, mall_ref, lvl_ref,
                     out_ref, st_ref):
    @pl.when(pl.program_id(2) == 0)
    def _():
        st_ref[...] = jnp.zeros_like(st_ref)

    scale = GLA_HEAD_K ** -0.5
    rows = lambda c: slice(c * BLK, (c + 1) * BLK)

    def gates_fn(c):
        z = jnp.dot(lr_ref[0, rows(c), :], wup_ref[...], preferred_element_type=jnp.float32) + bup_ref[...]
        logg = (jnp.minimum(z, 0.0) - jnp.log(1.0 + jnp.exp(-jnp.abs(z)))) * (LOG2E / GLA_TAU)
        return logg, [k_ref[hh, rows(c), :].astype(jnp.float32) for hh in range(2)]

    def q_fn(c, hh):
        return q_ref[hh, rows(c), :].astype(jnp.float32) * scale

    def v_fn(c, hh):
        return jnp.concatenate([v_ref[2 * hh, rows(c), :], v_ref[2 * hh + 1, rows(c), :]], axis=-1)

    def out_fn(c, hh, o):
        gate = jnp.concatenate([g_ref[2 * hh, rows(c), :], g_ref[2 * hh + 1, rows(c), :]],
                               axis=-1).astype(jnp.float32)
        cols = slice(hh * GLA_HEAD_V, (hh + 1) * GLA_HEAD_V)
        out_ref[rows(c), cols] = _head_out(o, gate, gain_ref[:, cols]).astype(out_ref.dtype)

    _pair_pipeline(out_ref.shape[0] // BLK, gates_fn, q_fn, v_fn, out_fn, mall_ref, lvl_ref, st_ref)


def _hgrn_pair_kernel(q_ref, f_ref, i_ref, g_ref, lb_ref, gain_ref, mall_ref, lvl_ref, out_ref, st_ref):
    @pl.when(pl.program_id(2) == 0)
    def _():
        st_ref[...] = jnp.zeros_like(st_ref)

    scale = HGRN_HEAD_K ** -0.5
    rows = lambda c: slice(c * BLK, (c + 1) * BLK)
    lg = lb_ref[...]
    e = jnp.exp(lg - jnp.max(lg, axis=0, keepdims=True))
    lb = e[0:1, :] / jnp.sum(e, axis=0, keepdims=True)

    def gates_fn(c):
        hf = jnp.concatenate([f_ref[0, rows(c), :], f_ref[1, rows(c), :]], axis=-1).astype(jnp.float32)
        f = lb + (1.0 - lb) * (1.0 / (1.0 + jnp.exp(-hf)))
        kk = 1.0 - f
        return jnp.log(f) * LOG2E, [kk[:, hh * LANE:(hh + 1) * LANE] for hh in range(2)]

    def q_fn(c, hh):
        return q_ref[hh, rows(c), :].astype(jnp.float32) * scale

    def v_fn(c, hh):
        return i_ref[hh, rows(c), :]

    def out_fn(c, hh, o):
        gate = g_ref[hh, rows(c), :].astype(jnp.float32)
        cols = slice(hh * HGRN_HEAD_V, (hh + 1) * HGRN_HEAD_V)
        out_ref[rows(c), cols] = _head_out(o, gate, gain_ref[:, cols]).astype(out_ref.dtype)

    _pair_pipeline(out_ref.shape[0] // BLK, gates_fn, q_fn, v_fn, out_fn, mall_ref, lvl_ref, st_ref)


def _mixer_call(kernel, name, proj3, col_specs, extra, extra_specs, n_pairs, dv, batch, seq, tb):
    nt = seq // tb
    mall = jnp.asarray(_plane_matrix(), dtype=jnp.bfloat16)
    lvl = jnp.asarray(_level_matrix())
    const = lambda shape: pl.BlockSpec(shape, lambda b, p, t: (0,) * len(shape))
    in_specs = [pl.BlockSpec((nblk, tb, LANE), functools.partial(
        lambda b, p, t, nblk, first: (first // nblk + p, b * nt + t, 0), nblk=nblk, first=first))
        for nblk, first in col_specs]
    return pl.pallas_call(
        kernel,
        grid=(batch, n_pairs, nt),
        in_specs=in_specs + extra_specs + [const(mall.shape), const(lvl.shape)],
        out_specs=pl.BlockSpec((tb, 2 * dv), lambda b, p, t: (b * nt + t, p)),
        out_shape=jax.ShapeDtypeStruct((batch * seq, n_pairs * 2 * dv), jnp.bfloat16),
        scratch_shapes=[pltpu.VMEM((2, dv, LANE), jnp.float32)],
        compiler_params=pltpu.CompilerParams(
            dimension_semantics=("arbitrary", "arbitrary", "arbitrary"),
            vmem_limit_bytes=VMEM_LIMIT),
        name=name,
    )(*([proj3] * len(col_specs)), *extra, mall, lvl)


def _out_kernel(ma_ref, mb_ref, wa_ref, wb_ref, x_ref, fw_ref, out_ref):
    y = jnp.dot(ma_ref[...], wa_ref[...], preferred_element_type=jnp.float32)
    y = y + jnp.dot(mb_ref[...], wb_ref[...], preferred_element_type=jnp.float32)
    y = y + x_ref[...]
    ms = jnp.mean(y * y, axis=-1, keepdims=True)
    out_ref[...] = y * lax.rsqrt(ms + EPS) * fw_ref[...]


def _output_projection(ma, mb, wa, wb, x2, final_w, tm=512):
    bt = x2.shape[0]
    return pl.pallas_call(
        _out_kernel,
        grid=(bt // tm,),
        in_specs=[
            pl.BlockSpec((tm, GLA_WIDTH), lambda i: (i, 0)),
            pl.BlockSpec((tm, HGRN_WIDTH), lambda i: (i, 0)),
            pl.BlockSpec((GLA_WIDTH, D_MODEL), lambda i: (0, 0)),
            pl.BlockSpec((HGRN_WIDTH, D_MODEL), lambda i: (0, 0)),
            pl.BlockSpec((tm, D_MODEL), lambda i: (i, 0)),
            pl.BlockSpec((1, D_MODEL), lambda i: (0, 0)),
        ],
        out_specs=pl.BlockSpec((tm, D_MODEL), lambda i: (i, 0)),
        out_shape=jax.ShapeDtypeStruct((bt, D_MODEL), jnp.float32),
        compiler_params=pltpu.CompilerParams(
            dimension_semantics=("arbitrary",),
            vmem_limit_bytes=VMEM_LIMIT),
        name="out_proj_residual_norm",
    )(ma, mb, wa, wb, x2, final_w)


def kernel(x, norm_w, w_in, gla_w_up, gla_b_up, gla_norm_w, hgrn_lb_logits, hgrn_norm_w, w_out, final_norm_w):
    batch, seq, d = x.shape
    assert d == D_MODEL and norm_w.shape[0] == 1 and seq % 1024 == 0
    x2 = x.reshape(batch * seq, d)
    w = w_in[0]
    lr0 = GLA_G0 * LANE + GLA_WIDTH
    w_all = jnp.concatenate(
        [w[:, :lr0], w[:, lr0 + GLA_LOWRANK:], w[:, lr0:lr0 + GLA_LOWRANK],
         jnp.zeros((d, LANE - GLA_LOWRANK), w.dtype)], axis=1).astype(jnp.bfloat16)

    proj3 = _input_projection(x2, norm_w, w_all)

    tb = 1024
    nt = seq // tb
    wup = jnp.pad(gla_w_up[0], ((0, LANE - GLA_LOWRANK), (0, 0))).astype(jnp.bfloat16)
    mixed_a = _mixer_call(
        _gla_pair_kernel, "gla_pair", proj3,
        [(2, GLA_Q0), (2, GLA_K0), (4, GLA_V0), (4, GLA_G0)],
        [proj3, wup, gla_b_up, gla_norm_w],
        [pl.BlockSpec((1, tb, LANE), lambda b, p, t: (LR_BLK, b * nt + t, 0)),
         pl.BlockSpec((LANE, 2 * LANE), lambda b, p, t: (0, p)),
         pl.BlockSpec((1, 2 * LANE), lambda b, p, t: (0, p)),
         pl.BlockSpec((1, 2 * GLA_HEAD_V), lambda b, p, t: (0, p))],
        GLA_HEADS // 2, GLA_HEAD_V, batch, seq, tb)
    mixed_b = _mixer_call(
        _hgrn_pair_kernel, "hgrn_pair", proj3,
        [(2, HG_Q0), (2, HG_F0), (2, HG_I0), (2, HG_G0)],
        [hgrn_lb_logits, hgrn_norm_w],
        [pl.BlockSpec((hgrn_lb_logits.shape[0], 2 * LANE), lambda b, p, t: (0, p)),
         pl.BlockSpec((1, 2 * HGRN_HEAD_V), lambda b, p, t: (0, p))],
        HGRN_HEADS // 2, HGRN_HEAD_V, batch, seq, tb)

    wo = w_out[0].astype(jnp.bfloat16)
    out = _output_projection(mixed_a, mixed_b, wo[:GLA_WIDTH], wo[GLA_WIDTH:], x2,
                             final_norm_w.reshape(1, d))
    return out.reshape(batch, seq, d)
```

```python
import functools

import numpy as np
import jax
import jax.numpy as jnp
from jax import lax
from jax.experimental import pallas as pl
from jax.experimental.pallas import tpu as pltpu

D_MODEL = 2048
GLA_HEADS = 4
GLA_HEAD_K = 128
GLA_HEAD_V = 256
GLA_KEY_WIDTH = GLA_HEADS * GLA_HEAD_K
GLA_WIDTH = GLA_HEADS * GLA_HEAD_V
GLA_LOWRANK = 16
GLA_TAU = 16.0
HGRN_HEADS = 8
HGRN_HEAD_K = 128
HGRN_HEAD_V = 128
HGRN_WIDTH = HGRN_HEADS * HGRN_HEAD_V
EPS = 1e-6

LANE = 128
SUBLANE = 8
BLK = 128
LEVELS = (64, 32, 16, 8, 4, 2, 1)
N_PLANES = len(LEVELS) + 2
DIAG_LEVEL = len(LEVELS)
VMEM_LIMIT = 56 * 1024 * 1024
LOG2E = 1.4426950408889634

GLA_Q0, GLA_K0, GLA_V0, GLA_G0 = 0, 4, 8, 16
HG_Q0, HG_F0, HG_I0, HG_G0 = 24, 32, 40, 48
LR_BLK = 56
LR0 = GLA_G0 * LANE + GLA_WIDTH
PROJ_CHUNK = 512
OUT_ROWS = 256


def _plane_matrix():
    r = np.arange(BLK)[:, None]
    t = np.arange(BLK)[None, :]
    planes = []
    for s in LEVELS:
        start = (r // (2 * s)) * (2 * s)
        mid = start + s - 1
        upper = (r - start) >= s
        planes.append(np.where(upper, (t > mid) & (t <= r), (t > r) & (t <= mid)))
    planes.append(t <= r)
    planes.append(t > r)
    return np.concatenate(planes, axis=0).astype(np.float32)


def _level_matrix():
    i = np.arange(BLK)[:, None]
    j = np.arange(BLK)[None, :]
    x = i ^ j
    msb = np.zeros_like(x)
    for bit in range(7):
        msb = np.where(x >> bit > 0, bit, msb)
    lvl = (len(LEVELS) - 1) - msb
    lvl = np.where(i == j, DIAG_LEVEL, lvl)
    lvl = np.where(i < j, -1, lvl)
    return lvl.astype(np.int32)


def _wprep_kernel(w_ref, tail_ref, out_ref):
    tr = out_ref.shape[0]
    lane = lax.broadcasted_iota(jnp.int32, (tr, LANE), 1)
    out_ref[:, :LR0] = w_ref[0, :, :LR0].astype(out_ref.dtype)
    xa = w_ref[0, :, LR0:LR_BLK * LANE]
    n = xa.shape[1]
    shifted = pltpu.roll(xa, n - GLA_LOWRANK, axis=1)
    out_ref[:, LR0:LR0 + n - LANE] = shifted[:, :n - LANE].astype(out_ref.dtype)
    last = jnp.where(lane < LANE - GLA_LOWRANK, shifted[:, n - LANE:], tail_ref[...])
    out_ref[:, LR0 + n - LANE:LR0 + n] = last.astype(out_ref.dtype)
    out_ref[:, LR0 + n:] = jnp.where(lane < GLA_LOWRANK, xa[:, :LANE], 0.0).astype(out_ref.dtype)


def _prepare_in_weight(w_in, tr=256):
    d, n_in = w_in.shape[1], w_in.shape[2]
    n_out = (LR_BLK + 1) * LANE
    assert n_in == LR_BLK * LANE + GLA_LOWRANK
    tail = w_in[0, :, n_in - LANE:]
    return pl.pallas_call(
        _wprep_kernel,
        grid=(d // tr,),
        in_specs=[pl.BlockSpec((1, tr, n_in), lambda i: (0, i, 0)),
                  pl.BlockSpec((tr, LANE), lambda i: (i, 0))],
        out_specs=pl.BlockSpec((tr, n_out), lambda i: (i, 0)),
        out_shape=jax.ShapeDtypeStruct((d, n_out), jnp.bfloat16),
        compiler_params=pltpu.CompilerParams(
            dimension_semantics=("arbitrary",),
            vmem_limit_bytes=VMEM_LIMIT),
        name="in_weight_prep",
    )(w_in, tail)


def _proj_kernel(x_ref, nw_ref, w_ref, out_ref):
    x = x_ref[...]
    ms = jnp.mean(x * x, axis=-1, keepdims=True)
    h = (x * lax.rsqrt(ms + EPS) * nw_ref[...]).astype(jnp.bfloat16)
    n = w_ref.shape[1]
    for j0 in range(0, n, PROJ_CHUNK):
        j1 = min(j0 + PROJ_CHUNK, n)
        acc = jnp.dot(h, w_ref[:, j0:j1], preferred_element_type=jnp.float32)
        for c in range((j1 - j0) // LANE):
            out_ref[j0 // LANE + c] = acc[:, c * LANE:(c + 1) * LANE].astype(out_ref.dtype)


def _input_projection(x2, norm_w, w_all, tm=256):
    bt = x2.shape[0]
    n = w_all.shape[1]
    return pl.pallas_call(
        _proj_kernel,
        grid=(bt // tm,),
        in_specs=[
            pl.BlockSpec((tm, D_MODEL), lambda i: (i, 0)),
            pl.BlockSpec((1, D_MODEL), lambda i: (0, 0)),
            pl.BlockSpec((D_MODEL, n), lambda i: (0, 0), pipeline_mode=pl.Buffered(1)),
        ],
        out_specs=pl.BlockSpec((n // LANE, tm, LANE), lambda i: (0, i, 0)),
        out_shape=jax.ShapeDtypeStruct((n // LANE, bt, LANE), jnp.bfloat16),
        compiler_params=pltpu.CompilerParams(
            dimension_semantics=("arbitrary",),
            vmem_limit_bytes=VMEM_LIMIT),
        name="rmsnorm_in_proj",
    )(x2, norm_w, w_all)


def _upper_runs(x, s):
    return jnp.concatenate([x[r:r + s] for r in range(s, BLK, 2 * s)], axis=0)


def _prep_block(q, k, planes, hh):
    lanes = slice(hh * LANE, (hh + 1) * LANE)
    row = lax.broadcasted_iota(jnp.int32, (BLK, LANE), 0)
    lhs, w = [], []
    for li, s in enumerate(LEVELS):
        f = jnp.exp2(planes[li * BLK:(li + 1) * BLK, lanes])
        if s >= SUBLANE:
            qk = jnp.concatenate([(q if (r & s) else k)[r:r + SUBLANE] for r in range(0, BLK, SUBLANE)], axis=0)
            p32 = qk * f
            lhs.append(_upper_runs(p32, s).astype(jnp.bfloat16))
        else:
            p32 = jnp.where((row & s) != 0, q, k) * f
            lhs.append(None)
        w.append(p32.astype(jnp.bfloat16))
    e_b = planes[len(LEVELS) * BLK:(len(LEVELS) + 1) * BLK, lanes]
    e_dec = planes[(len(LEVELS) + 1) * BLK:(len(LEVELS) + 2) * BLK, lanes]
    return dict(
        lhs=lhs, w=w,
        q_in=(q * jnp.exp2(e_b)).astype(jnp.bfloat16),
        k_dec=(k * jnp.exp2(e_dec)).astype(jnp.bfloat16),
        decay=jnp.exp2(e_b[BLK - 1:BLK, :]),
        dcol=jnp.sum(q * k, axis=-1, keepdims=True))


def _level_scores(ops):
    out = [[], []]
    for li in range(len(LEVELS)):
        w2 = jnp.concatenate([ops[0]['w'][li], ops[1]['w'][li]], axis=1)
        wt = lax.bitcast_convert_type(lax.bitcast_convert_type(w2, jnp.uint16).T, jnp.bfloat16)
        for hh in range(2):
            lhs = ops[hh]['lhs'][li]
            lhs = ops[hh]['w'][li] if lhs is None else lhs
            out[hh].append(jnp.dot(lhs, wt[hh * LANE:(hh + 1) * LANE], preferred_element_type=jnp.float32))
    return out


def _assemble_scores(g, dcol, lvl_ref):
    groups = []
    for r in range(0, BLK, SUBLANE):
        lvl = lvl_ref[r:r + SUBLANE, :]
        a = jnp.where(lvl == DIAG_LEVEL, dcol[r:r + SUBLANE], 0.0)
        for li, s in enumerate(LEVELS):
            if s >= SUBLANE:
                if r & s:
                    k = (r // (2 * s)) * s + (r % s)
                    a = jnp.where(lvl == li, g[li][k:k + SUBLANE], a)
            else:
                a = jnp.where(lvl == li, g[li][r:r + SUBLANE], a)
        groups.append(a)
    return jnp.concatenate(groups, axis=0).astype(jnp.bfloat16)


def _finish_block(a, ops, v_bf, st_ref, hh):
    st = st_ref[hh]
    o = jnp.dot(a, v_bf, preferred_element_type=jnp.float32)
    o = o + jnp.dot(ops['q_in'], st.astype(jnp.bfloat16).T, preferred_element_type=jnp.float32)
    upd = jnp.dot(v_bf.T, ops['k_dec'], preferred_element_type=jnp.float32)
    st_ref[hh] = st * ops['decay'] + upd
    return o


def _head_out(o, gate, gain):
    ms = jnp.mean(o * o, axis=-1, keepdims=True)
    y = o * lax.rsqrt(ms + EPS) * gain
    return y * (gate / (1.0 + jnp.exp(-gate)))


def _pair_pipeline(nblk, gates_fn, q_fn, v_fn, out_fn, mall_ref, lvl_ref, st_ref):
    def exponents(c):
        logg, ks = gates_fn(c)
        return jnp.dot(mall_ref[...], logg.astype(jnp.bfloat16), preferred_element_type=jnp.float32), ks

    def operands(c, planes_ks):
        planes, ks = planes_ks
        return [_prep_block(q_fn(c, hh), ks[hh], planes, hh) for hh in range(2)]

    planes = {0: exponents(0)}
    if nblk > 1:
        planes[1] = exponents(1)
    ops = {0: operands(0, planes.pop(0))}
    for c in range(nblk):
        g = _level_scores(ops[c])
        if c + 2 < nblk:
            planes[c + 2] = exponents(c + 2)
        a = [_assemble_scores(g[hh], ops[c][hh]['dcol'], lvl_ref) for hh in range(2)]
        if c + 1 < nblk:
            ops[c + 1] = operands(c + 1, planes.pop(c + 1))
        cur = ops.pop(c)
        for hh in range(2):
            out_fn(c, hh, _finish_block(a[hh], cur[hh], v_fn(c, hh), st_ref, hh))


def _gla_pair_kernel(q_ref, k_ref, v_ref, g_ref, lr_ref, wup_ref, bup_ref, gain_ref, mall_ref, lvl_ref,
                     out_ref, st_ref):
    @pl.when(pl.program_id(2) == 0)
    def _():
        st_ref[...] = jnp.zeros_like(st_ref)

    scale = GLA_HEAD_K ** -0.5
    rows = lambda c: slice(c * BLK, (c + 1) * BLK)

    def gates_fn(c):
        z = jnp.dot(lr_ref[0, rows(c), :], wup_ref[...], preferred_element_type=jnp.float32) + bup_ref[...]
        logg = (jnp.minimum(z, 0.0) - jnp.log(1.0 + jnp.exp(-jnp.abs(z)))) * (LOG2E / GLA_TAU)
        return logg, [k_ref[hh, rows(c), :].astype(jnp.float32) for hh in range(2)]

    def q_fn(c, hh):
        return q_ref[hh, rows(c), :].astype(jnp.float32) * scale

    def v_fn(c, hh):
        return jnp.concatenate([v_ref[2 * hh, rows(c), :], v_ref[2 * hh + 1, rows(c), :]], axis=-1)

    def out_fn(c, hh, o):
        gate = jnp.concatenate([g_ref[2 * hh, rows(c), :], g_ref[2 * hh + 1, rows(c), :]],
                               axis=-1).astype(jnp.float32)
        cols = slice(hh * GLA_HEAD_V, (hh + 1) * GLA_HEAD_V)
        out_ref[rows(c), cols] = _head_out(o, gate, gain_ref[:, cols]).astype(out_ref.dtype)

    _pair_pipeline(out_ref.shape[0] // BLK, gates_fn, q_fn, v_fn, out_fn, mall_ref, lvl_ref, st_ref)


def _hgrn_pair_kernel(q_ref, f_ref, i_ref, g_ref, lb_ref, gain_ref, mall_ref, lvl_ref, out_ref, st_ref):
    @pl.when(pl.program_id(2) == 0)
    def _():
        st_ref[...] = jnp.zeros_like(st_ref)

    scale = HGRN_HEAD_K ** -0.5
    rows = lambda c: slice(c * BLK, (c + 1) * BLK)
    lg = lb_ref[...]
    e = jnp.exp(lg - jnp.max(lg, axis=0, keepdims=True))
    lb = e[0:1, :] / jnp.sum(e, axis=0, keepdims=True)

    def gates_fn(c):
        hf = jnp.concatenate([f_ref[0, rows(c), :], f_ref[1, rows(c), :]], axis=-1).astype(jnp.float32)
        f = lb + (1.0 - lb) * (1.0 / (1.0 + jnp.exp(-hf)))
        kk = 1.0 - f
        return jnp.log(f) * LOG2E, [kk[:, hh * LANE:(hh + 1) * LANE] for hh in range(2)]

    def q_fn(c, hh):
        return q_ref[hh, rows(c), :].astype(jnp.float32) * scale

    def v_fn(c, hh):
        return i_ref[hh, rows(c), :]

    def out_fn(c, hh, o):
        gate = g_ref[hh, rows(c), :].astype(jnp.float32)
        cols = slice(hh * HGRN_HEAD_V, (hh + 1) * HGRN_HEAD_V)
        out_ref[rows(c), cols] = _head_out(o, gate, gain_ref[:, cols]).astype(out_ref.dtype)

    _pair_pipeline(out_ref.shape[0] // BLK, gates_fn, q_fn, v_fn, out_fn, mall_ref, lvl_ref, st_ref)


def _mixer_call(kernel, name, proj3, col_specs, extra, extra_specs, n_pairs, dv, batch, seq, tb):
    nt = seq // tb
    mall = jnp.asarray(_plane_matrix(), dtype=jnp.bfloat16)
    lvl = jnp.asarray(_level_matrix())
    const = lambda shape: pl.BlockSpec(shape, lambda b, p, t: (0,) * len(shape))
    in_specs = [pl.BlockSpec((nblk, tb, LANE), functools.partial(
        lambda b, p, t, nblk, first: (first // nblk + p, b * nt + t, 0), nblk=nblk, first=first))
        for nblk, first in col_specs]
    return pl.pallas_call(
        kernel,
        grid=(batch, n_pairs, nt),
        in_specs=in_specs + extra_specs + [const(mall.shape), const(lvl.shape)],
        out_specs=pl.BlockSpec((tb, 2 * dv), lambda b, p, t: (b * nt + t, p)),
        out_shape=jax.ShapeDtypeStruct((batch * seq, n_pairs * 2 * dv), jnp.bfloat16),
        scratch_shapes=[pltpu.VMEM((2, dv, LANE), jnp.float32)],
        compiler_params=pltpu.CompilerParams(
            dimension_semantics=("arbitrary", "arbitrary", "arbitrary"),
            vmem_limit_bytes=VMEM_LIMIT),
        name=name,
    )(*([proj3] * len(col_specs)), *extra, mall, lvl)


def _out_kernel(ma_ref, mb_ref, w_ref, x_ref, fw_ref, out_ref):
    for r0 in range(0, out_ref.shape[0], OUT_ROWS):
        rows = slice(r0, r0 + OUT_ROWS)
        m = jnp.concatenate([ma_ref[rows, :], mb_ref[rows, :]], axis=-1)
        y = jnp.dot(m, w_ref[...], preferred_element_type=jnp.float32) + x_ref[rows, :]
        ms = jnp.mean(y * y, axis=-1, keepdims=True)
        out_ref[rows, :] = y * lax.rsqrt(ms + EPS) * fw_ref[...]


def _output_projection(ma, mb, wo, x2, final_w, tm=1024):
    bt = x2.shape[0]
    return pl.pallas_call(
        _out_kernel,
        grid=(bt // tm,),
        in_specs=[
            pl.BlockSpec((tm, GLA_WIDTH), lambda i: (i, 0)),
            pl.BlockSpec((tm, HGRN_WIDTH), lambda i: (i, 0)),
            pl.BlockSpec((GLA_WIDTH + HGRN_WIDTH, D_MODEL), lambda i: (0, 0), pipeline_mode=pl.Buffered(1)),
            pl.BlockSpec((tm, D_MODEL), lambda i: (i, 0)),
            pl.BlockSpec((1, D_MODEL), lambda i: (0, 0)),
        ],
        out_specs=pl.BlockSpec((tm, D_MODEL), lambda i: (i, 0)),
        out_shape=jax.ShapeDtypeStruct((bt, D_MODEL), jnp.float32),
        compiler_params=pltpu.CompilerParams(
            dimension_semantics=("arbitrary",),
            vmem_limit_bytes=VMEM_LIMIT),
        name="out_proj_residual_norm",
    )(ma, mb, wo, x2, final_w)


def kernel(x, norm_w, w_in, gla_w_up, gla_b_up, gla_norm_w, hgrn_lb_logits, hgrn_norm_w, w_out, final_norm_w):
    batch, seq, d = x.shape
    assert d == D_MODEL and norm_w.shape[0] == 1 and seq % 1024 == 0
    x2 = x.reshape(batch * seq, d)
    proj3 = _input_projection(x2, norm_w, _prepare_in_weight(w_in))

    tb = 1024
    nt = seq // tb
    wup = jnp.pad(gla_w_up[0], ((0, LANE - GLA_LOWRANK), (0, 0))).astype(jnp.bfloat16)
    mixed_a = _mixer_call(
        _gla_pair_kernel, "gla_pair", proj3,
        [(2, GLA_Q0), (2, GLA_K0), (4, GLA_V0), (4, GLA_G0)],
        [proj3, wup, gla_b_up, gla_norm_w],
        [pl.BlockSpec((1, tb, LANE), lambda b, p, t: (LR_BLK, b * nt + t, 0)),
         pl.BlockSpec((LANE, 2 * LANE), lambda b, p, t: (0, p)),
         pl.BlockSpec((1, 2 * LANE), lambda b, p, t: (0, p)),
         pl.BlockSpec((1, 2 * GLA_HEAD_V), lambda b, p, t: (0, p))],
        GLA_HEADS // 2, GLA_HEAD_V, batch, seq, tb)
    mixed_b = _mixer_call(
        _hgrn_pair_kernel, "hgrn_pair", proj3,
        [(2, HG_Q0), (2, HG_F0), (2, HG_I0), (2, HG_G0)],
        [hgrn_lb_logits, hgrn_norm_w],
        [pl.BlockSpec((hgrn_lb_logits.shape[0], 2 * LANE), lambda b, p, t: (0, p)),
         pl.BlockSpec((1, 2 * HGRN_HEAD_V), lambda b, p, t: (0, p))],
        HGRN_HEADS // 2, HGRN_HEAD_V, batch, seq, tb)

    out = _output_projection(mixed_a, mixed_b, w_out[0].astype(jnp.bfloat16), x2, final_norm_w.reshape(1, d))
    return out.reshape(batch, seq, d)
```

```python
import functools

import numpy as np
import jax
import jax.numpy as jnp
from jax import lax
from jax.experimental import pallas as pl
from jax.experimental.pallas import tpu as pltpu

D_MODEL = 2048
GLA_HEADS = 4
GLA_HEAD_K = 128
GLA_HEAD_V = 256
GLA_KEY_WIDTH = GLA_HEADS * GLA_HEAD_K
GLA_WIDTH = GLA_HEADS * GLA_HEAD_V
GLA_LOWRANK = 16
GLA_TAU = 16.0
HGRN_HEADS = 8
HGRN_HEAD_K = 128
HGRN_HEAD_V = 128
HGRN_WIDTH = HGRN_HEADS * HGRN_HEAD_V
EPS = 1e-6

LANE = 128
SUBLANE = 8
BLK = 128
LEVELS = (64, 32, 16, 8, 4, 2, 1)
N_PLANES = len(LEVELS) + 2
DIAG_LEVEL = len(LEVELS)
VMEM_LIMIT = 56 * 1024 * 1024
LOG2E = 1.4426950408889634

GLA_Q0, GLA_K0, GLA_V0, GLA_G0 = 0, 4, 8, 16
HG_Q0, HG_F0, HG_I0, HG_G0 = 24, 32, 40, 48
LR_BLK = 56
LR0 = GLA_G0 * LANE + GLA_WIDTH
PROJ_CHUNK = 512
WPREP_COLS = 512
OUT_ROWS = 256


def _plane_matrix():
    r = np.arange(BLK)[:, None]
    t = np.arange(BLK)[None, :]
    planes = []
    for s in LEVELS:
        start = (r // (2 * s)) * (2 * s)
        mid = start + s - 1
        upper = (r - start) >= s
        planes.append(np.where(upper, (t > mid) & (t <= r), (t > r) & (t <= mid)))
    planes.append(t <= r)
    planes.append(t > r)
    return np.concatenate(planes, axis=0).astype(np.float32)


def _level_matrix():
    i = np.arange(BLK)[:, None]
    j = np.arange(BLK)[None, :]
    x = i ^ j
    msb = np.zeros_like(x)
    for bit in range(7):
        msb = np.where(x >> bit > 0, bit, msb)
    lvl = (len(LEVELS) - 1) - msb
    lvl = np.where(i == j, DIAG_LEVEL, lvl)
    lvl = np.where(i < j, -1, lvl)
    return lvl.astype(np.int32)


def _wprep_kernel(wt_ref, out_ref):
    j = pl.program_id(0)
    xt = wt_ref[...].astype(out_ref.dtype).T
    lane = lax.broadcasted_iota(jnp.int32, xt.shape, 1)
    keep = jnp.logical_or(j < LR_BLK * LANE // WPREP_COLS, lane < GLA_LOWRANK)
    out_ref[...] = jnp.where(keep, xt, jnp.zeros_like(xt))


def _prepare_in_weight(w_in):
    d, n_in = w_in.shape[1], w_in.shape[2]
    assert n_in == LR_BLK * LANE + GLA_LOWRANK and LR0 % WPREP_COLS == 0
    n_main = LR_BLK * LANE // WPREP_COLS

    def src_row(j):
        shifted = j * WPREP_COLS + GLA_LOWRANK
        row = jnp.where(j < LR0 // WPREP_COLS, j * WPREP_COLS, jnp.where(j < n_main, shifted, LR0))
        return pl.multiple_of(row, GLA_LOWRANK)

    return pl.pallas_call(
        _wprep_kernel,
        grid=(n_main + 1,),
        in_specs=[pl.BlockSpec((pl.Element(WPREP_COLS), pl.Element(d)), lambda j: (src_row(j), 0))],
        out_specs=pl.BlockSpec((d, WPREP_COLS), lambda j: (0, j)),
        out_shape=jax.ShapeDtypeStruct((d, (LR_BLK + 1) * LANE), jnp.bfloat16),
        compiler_params=pltpu.CompilerParams(
            dimension_semantics=("arbitrary",),
            vmem_limit_bytes=VMEM_LIMIT),
        name="in_weight_prep",
    )(w_in[0].T)


def _proj_kernel(x_ref, nw_ref, w_ref, out_ref):
    x = x_ref[...]
    ms = jnp.mean(x * x, axis=-1, keepdims=True)
    h = (x * lax.rsqrt(ms + EPS) * nw_ref[...]).astype(jnp.bfloat16)
    n = w_ref.shape[1]
    for j0 in range(0, n, PROJ_CHUNK):
        j1 = min(j0 + PROJ_CHUNK, n)
        acc = jnp.dot(h, w_ref[:, j0:j1], preferred_element_type=jnp.float32)
        for c in range((j1 - j0) // LANE):
            out_ref[j0 // LANE + c] = acc[:, c * LANE:(c + 1) * LANE].astype(out_ref.dtype)


def _input_projection(x2, norm_w, w_all, tm=256):
    bt = x2.shape[0]
    n = w_all.shape[1]
    return pl.pallas_call(
        _proj_kernel,
        grid=(bt // tm,),
        in_specs=[
            pl.BlockSpec((tm, D_MODEL), lambda i: (i, 0)),
            pl.BlockSpec((1, D_MODEL), lambda i: (0, 0)),
            pl.BlockSpec((D_MODEL, n), lambda i: (0, 0), pipeline_mode=pl.Buffered(1)),
        ],
        out_specs=pl.BlockSpec((n // LANE, tm, LANE), lambda i: (0, i, 0)),
        out_shape=jax.ShapeDtypeStruct((n // LANE, bt, LANE), jnp.bfloat16),
        compiler_params=pltpu.CompilerParams(
            dimension_semantics=("arbitrary",),
            vmem_limit_bytes=VMEM_LIMIT),
        name="rmsnorm_in_proj",
    )(x2, norm_w, w_all)


def _upper_runs(x, s):
    return jnp.concatenate([x[r:r + s] for r in range(s, BLK, 2 * s)], axis=0)


def _prep_block(q, k, planes, hh):
    lanes = slice(hh * LANE, (hh + 1) * LANE)
    row = lax.broadcasted_iota(jnp.int32, (BLK, LANE), 0)
    lhs, w = [], []
    for li, s in enumerate(LEVELS):
        f = jnp.exp2(planes[li * BLK:(li + 1) * BLK, lanes])
        if s >= SUBLANE:
            qk = jnp.concatenate([(q if (r & s) else k)[r:r + SUBLANE] for r in range(0, BLK, SUBLANE)], axis=0)
            p32 = qk * f
            lhs.append(_upper_runs(p32, s).astype(jnp.bfloat16))
        else:
            p32 = jnp.where((row & s) != 0, q, k) * f
            lhs.append(None)
        w.append(p32.astype(jnp.bfloat16))
    e_b = planes[len(LEVELS) * BLK:(len(LEVELS) + 1) * BLK, lanes]
    e_dec = planes[(len(LEVELS) + 1) * BLK:(len(LEVELS) + 2) * BLK, lanes]
    return dict(
        lhs=lhs, w=w,
        q_in=(q * jnp.exp2(e_b)).astype(jnp.bfloat16),
        k_dec=(k * jnp.exp2(e_dec)).astype(jnp.bfloat16),
        decay=jnp.exp2(e_b[BLK - 1:BLK, :]),
        dcol=jnp.sum(q * k, axis=-1, keepdims=True))


def _level_scores(ops):
    out = [[], []]
    for li in range(len(LEVELS)):
        w2 = jnp.concatenate([ops[0]['w'][li], ops[1]['w'][li]], axis=1)
        wt = lax.bitcast_convert_type(lax.bitcast_convert_type(w2, jnp.uint16).T, jnp.bfloat16)
        for hh in range(2):
            lhs = ops[hh]['lhs'][li]
            lhs = ops[hh]['w'][li] if lhs is None else lhs
            out[hh].append(jnp.dot(lhs, wt[hh * LANE:(hh + 1) * LANE], preferred_element_type=jnp.float32))
    return out


def _assemble_scores(g, dcol, lvl_ref):
    groups = []
    for r in range(0, BLK, SUBLANE):
        lvl = lvl_ref[r:r + SUBLANE, :]
        a = jnp.where(lvl == DIAG_LEVEL, dcol[r:r + SUBLANE], 0.0)
        for li, s in enumerate(LEVELS):
            if s >= SUBLANE:
                if r & s:
                    k = (r // (2 * s)) * s + (r % s)
                    a = jnp.where(lvl == li, g[li][k:k + SUBLANE], a)
            else:
                a = jnp.where(lvl == li, g[li][r:r + SUBLANE], a)
        groups.append(a)
    return jnp.concatenate(groups, axis=0).astype(jnp.bfloat16)


def _finish_block(a, ops, v_bf, st_ref, hh):
    st = st_ref[hh]
    o = jnp.dot(a, v_bf, preferred_element_type=jnp.float32)
    o = o + jnp.dot(ops['q_in'], st.astype(jnp.bfloat16).T, preferred_element_type=jnp.float32)
    upd = jnp.dot(v_bf.T, ops['k_dec'], preferred_element_type=jnp.float32)
    st_ref[hh] = st * ops['decay'] + upd
    return o


def _head_out(o, gate, gain):
    ms = jnp.mean(o * o, axis=-1, keepdims=True)
    y = o * lax.rsqrt(ms + EPS) * gain
    return y * (gate / (1.0 + jnp.exp(-gate)))


def _pair_pipeline(nblk, gates_fn, q_fn, v_fn, out_fn, mall_ref, lvl_ref, st_ref):
    def exponents(c):
        logg, ks = gates_fn(c)
        return jnp.dot(mall_ref[...], logg.astype(jnp.bfloat16), preferred_element_type=jnp.float32), ks

    def operands(c, planes_ks):
        planes, ks = planes_ks
        return [_prep_block(q_fn(c, hh), ks[hh], planes, hh) for hh in range(2)]

    planes = {0: exponents(0)}
    if nblk > 1:
        planes[1] = exponents(1)
    ops = {0: operands(0, planes.pop(0))}
    for c in range(nblk):
        g = _level_scores(ops[c])
        if c + 2 < nblk:
            planes[c + 2] = exponents(c + 2)
        a = [_assemble_scores(g[hh], ops[c][hh]['dcol'], lvl_ref) for hh in range(2)]
        if c + 1 < nblk:
            ops[c + 1] = operands(c + 1, planes.pop(c + 1))
        cur = ops.pop(c)
        for hh in range(2):
            out_fn(c, hh, _finish_block(a[hh], cur[hh], v_fn(c, hh), st_ref, hh))


def _gla_pair_kernel(q_ref, k_ref, v_ref, g_ref, lr_ref, wup_ref, bup_ref, gain_ref, mall_ref, lvl_ref,
                     out_ref, st_ref):
    @pl.when(pl.program_id(2) == 0)
    def _():
        st_ref[...] = jnp.zeros_like(st_ref)

    scale = GLA_HEAD_K ** -0.5
    rows = lambda c: slice(c * BLK, (c + 1) * BLK)

    def gates_fn(c):
        z = jnp.dot(lr_ref[0, rows(c), :], wup_ref[...], preferred_element_type=jnp.float32) + bup_ref[...]
        logg = (jnp.minimum(z, 0.0) - jnp.log(1.0 + jnp.exp(-jnp.abs(z)))) * (LOG2E / GLA_TAU)
        return logg, [k_ref[hh, rows(c), :].astype(jnp.float32) for hh in range(2)]

    def q_fn(c, hh):
        return q_ref[hh, rows(c), :].astype(jnp.float32) * scale

    def v_fn(c, hh):
        return jnp.concatenate([v_ref[2 * hh, rows(c), :], v_ref[2 * hh + 1, rows(c), :]], axis=-1)

    def out_fn(c, hh, o):
        gate = jnp.concatenate([g_ref[2 * hh, rows(c), :], g_ref[2 * hh + 1, rows(c), :]],
                               axis=-1).astype(jnp.float32)
        cols = slice(hh * GLA_HEAD_V, (hh + 1) * GLA_HEAD_V)
        out_ref[rows(c), cols] = _head_out(o, gate, gain_ref[:, cols]).astype(out_ref.dtype)

    _pair_pipeline(out_ref.shape[0] // BLK, gates_fn, q_fn, v_fn, out_fn, mall_ref, lvl_ref, st_ref)


def _hgrn_pair_kernel(q_ref, f_ref, i_ref, g_ref, lb_ref, gain_ref, mall_ref, lvl_ref, out_ref, st_ref):
    @pl.when(pl.program_id(2) == 0)
    def _():
        st_ref[...] = jnp.zeros_like(st_ref)

    scale = HGRN_HEAD_K ** -0.5
    rows = lambda c: slice(c * BLK, (c + 1) * BLK)
    lg = lb_ref[...]
    e = jnp.exp(lg - jnp.max(lg, axis=0, keepdims=True))
    lb = e[0:1, :] / jnp.sum(e, axis=0, keepdims=True)

    def gates_fn(c):
        hf = jnp.concatenate([f_ref[0, rows(c), :], f_ref[1, rows(c), :]], axis=-1).astype(jnp.float32)
        f = lb + (1.0 - lb) * (1.0 / (1.0 + jnp.exp(-hf)))
        kk = 1.0 - f
        return jnp.log(f) * LOG2E, [kk[:, hh * LANE:(hh + 1) * LANE] for hh in range(2)]

    def q_fn(c, hh):
        return q_ref[hh, rows(c), :].astype(jnp.float32) * scale

    def v_fn(c, hh):
        return i_ref[hh, rows(c), :]

    def out_fn(c, hh, o):
        gate = g_ref[hh, rows(c), :].astype(jnp.float32)
        cols = slice(hh * HGRN_HEAD_V, (hh + 1) * HGRN_HEAD_V)
        out_ref[rows(c), cols] = _head_out(o, gate, gain_ref[:, cols]).astype(out_ref.dtype)

    _pair_pipeline(out_ref.shape[0] // BLK, gates_fn, q_fn, v_fn, out_fn, mall_ref, lvl_ref, st_ref)


def _mixer_call(kernel, name, proj3, col_specs, extra, extra_specs, n_pairs, dv, batch, seq, tb):
    nt = seq // tb
    mall = jnp.asarray(_plane_matrix(), dtype=jnp.bfloat16)
    lvl = jnp.asarray(_level_matrix())
    const = lambda shape: pl.BlockSpec(shape, lambda b, p, t: (0,) * len(shape))
    in_specs = [pl.BlockSpec((nblk, tb, LANE), functools.partial(
        lambda b, p, t, nblk, first: (first // nblk + p, b * nt + t, 0), nblk=nblk, first=first))
        for nblk, first in col_specs]
    return pl.pallas_call(
        kernel,
        grid=(batch, n_pairs, nt),
        in_specs=in_specs + extra_specs + [const(mall.shape), const(lvl.shape)],
        out_specs=pl.BlockSpec((tb, 2 * dv), lambda b, p, t: (b * nt + t, p)),
        out_shape=jax.ShapeDtypeStruct((batch * seq, n_pairs * 2 * dv), jnp.bfloat16),
        scratch_shapes=[pltpu.VMEM((2, dv, LANE), jnp.float32)],
        compiler_params=pltpu.CompilerParams(
            dimension_semantics=("arbitrary", "arbitrary", "arbitrary"),
            vmem_limit_bytes=VMEM_LIMIT),
        name=name,
    )(*([proj3] * len(col_specs)), *extra, mall, lvl)


def _out_kernel(ma_ref, mb_ref, w_ref, x_ref, fw_ref, out_ref):
    for r0 in range(0, out_ref.shape[0], OUT_ROWS):
        rows = slice(r0, r0 + OUT_ROWS)
        m = jnp.concatenate([ma_ref[rows, :], mb_ref[rows, :]], axis=-1)
        y = jnp.dot(m, w_ref[...], preferred_element_type=jnp.float32) + x_ref[rows, :]
        ms = jnp.mean(y * y, axis=-1, keepdims=True)
        out_ref[rows, :] = y * lax.rsqrt(ms + EPS) * fw_ref[...]


def _output_projection(ma, mb, wo, x2, final_w, tm=1024):
    bt = x2.shape[0]
    return pl.pallas_call(
        _out_kernel,
        grid=(bt // tm,),
        in_specs=[
            pl.BlockSpec((tm, GLA_WIDTH), lambda i: (i, 0)),
            pl.BlockSpec((tm, HGRN_WIDTH), lambda i: (i, 0)),
            pl.BlockSpec((GLA_WIDTH + HGRN_WIDTH, D_MODEL), lambda i: (0, 0), pipeline_mode=pl.Buffered(1)),
            pl.BlockSpec((tm, D_MODEL), lambda i: (i, 0)),
            pl.BlockSpec((1, D_MODEL), lambda i: (0, 0)),
        ],
        out_specs=pl.BlockSpec((tm, D_MODEL), lambda i: (i, 0)),
        out_shape=jax.ShapeDtypeStruct((bt, D_MODEL), jnp.float32),
        compiler_params=pltpu.CompilerParams(
            dimension_semantics=("arbitrary",),
            vmem_limit_bytes=VMEM_LIMIT),
        name="out_proj_residual_norm",
    )(ma, mb, wo, x2, final_w)


def kernel(x, norm_w, w_in, gla_w_up, gla_b_up, gla_norm_w, hgrn_lb_logits, hgrn_norm_w, w_out, final_norm_w):
    batch, seq, d = x.shape
    assert d == D_MODEL and norm_w.shape[0] == 1 and seq % 1024 == 0
    x2 = x.reshape(batch * seq, d)
    proj3 = _input_projection(x2, norm_w, _prepare_in_weight(w_in))

    tb = 1024
    nt = seq // tb
    wup = jnp.pad(gla_w_up[0], ((0, LANE - GLA_LOWRANK), (0, 0))).astype(jnp.bfloat16)
    mixed_a = _mixer_call(
        _gla_pair_kernel, "gla_pair", proj3,
        [(2, GLA_Q0), (2, GLA_K0), (4, GLA_V0), (4, GLA_G0)],
        [proj3, wup, gla_b_up, gla_norm_w],
        [pl.BlockSpec((1, tb, LANE), lambda b, p, t: (LR_BLK, b * nt + t, 0)),
         pl.BlockSpec((LANE, 2 * LANE), lambda b, p, t: (0, p)),
         pl.BlockSpec((1, 2 * LANE), lambda b, p, t: (0, p)),
         pl.BlockSpec((1, 2 * GLA_HEAD_V), lambda b, p, t: (0, p))],
        GLA_HEADS // 2, GLA_HEAD_V, batch, seq, tb)
    mixed_b = _mixer_call(
        _hgrn_pair_kernel, "hgrn_pair", proj3,
        [(2, HG_Q0), (2, HG_F0), (2, HG_I0), (2, HG_G0)],
        [hgrn_lb_logits, hgrn_norm_w],
        [pl.BlockSpec((hgrn_lb_logits.shape[0], 2 * LANE), lambda b, p, t: (0, p)),
         pl.BlockSpec((1, 2 * HGRN_HEAD_V), lambda b, p, t: (0, p))],
        HGRN_HEADS // 2, HGRN_HEAD_V, batch, seq, tb)

    out = _output_projection(mixed_a, mixed_b, w_out[0].astype(jnp.bfloat16), x2, final_norm_w.reshape(1, d))
    return out.reshape(batch, seq, d)
```

```python
import functools

import numpy as np
import jax
import jax.numpy as jnp
from jax import lax
from jax.experimental import pallas as pl
from jax.experimental.pallas import tpu as pltpu

D_MODEL = 2048
GLA_HEADS = 4
GLA_HEAD_K = 128
GLA_HEAD_V = 256
GLA_KEY_WIDTH = GLA_HEADS * GLA_HEAD_K
GLA_WIDTH = GLA_HEADS * GLA_HEAD_V
GLA_LOWRANK = 16
GLA_TAU = 16.0
HGRN_HEADS = 8
HGRN_HEAD_K = 128
HGRN_HEAD_V = 128
HGRN_WIDTH = HGRN_HEADS * HGRN_HEAD_V
EPS = 1e-6

LANE = 128
SUBLANE = 8
BLK = 128
TILE = 2 * BLK
LEVELS = (64, 32, 16, 8, 4, 2, 1)
PLANE_LEVELS = (2, 4)
N_PLANES = len(PLANE_LEVELS) + 2
DIAG_LEVEL = len(LEVELS)
VMEM_LIMIT = 56 * 1024 * 1024
LOG2E = 1.4426950408889634

GLA_Q0, GLA_K0, GLA_V0, GLA_G0 = 0, 4, 8, 16
HG_Q0, HG_F0, HG_I0, HG_G0 = 24, 32, 40, 48
LR_BLK = 56
LR0 = GLA_G0 * LANE + GLA_WIDTH
PROJ_CHUNK = 512
WPREP_COLS = 512
LEAD_CHUNKS = 2
OUT_ROWS = 256


def _plane_matrix():
    r = np.arange(BLK)[:, None]
    t = np.arange(BLK)[None, :]
    planes = []
    for s in PLANE_LEVELS:
        start = (r // (2 * s)) * (2 * s)
        mid = start + s - 1
        upper = (r - start) >= s
        planes.append(np.where(upper, (t > mid) & (t <= r), (t > r) & (t <= mid)))
    start = (r // SUBLANE) * SUBLANE
    planes.append((t >= start) & (t <= r))
    planes.append((t > r) & (t < start + SUBLANE))
    return np.concatenate(planes, axis=0).astype(np.float32)


def _level_matrix():
    i = np.arange(BLK)[:, None]
    j = np.arange(BLK)[None, :]
    x = i ^ j
    msb = np.zeros_like(x)
    for bit in range(7):
        msb = np.where(x >> bit > 0, bit, msb)
    lvl = (len(LEVELS) - 1) - msb
    lvl = np.where(i == j, DIAG_LEVEL, lvl)
    lvl = np.where(i < j, -1, lvl)
    return lvl.astype(np.int32)


def _wprep_kernel(wt_ref, out_ref):
    j = pl.program_id(0)
    xt = wt_ref[...].astype(out_ref.dtype).T
    lane = lax.broadcasted_iota(jnp.int32, xt.shape, 1)
    keep = jnp.logical_or(j < LR_BLK * LANE // WPREP_COLS, lane < GLA_LOWRANK)
    out_ref[...] = jnp.where(keep, xt, jnp.zeros_like(xt))


def _prepare_in_weight(w_in):
    d, n_in = w_in.shape[1], w_in.shape[2]
    assert n_in == LR_BLK * LANE + GLA_LOWRANK and LR0 % WPREP_COLS == 0
    n_main = LR_BLK * LANE // WPREP_COLS

    def src_row(j):
        shifted = j * WPREP_COLS + GLA_LOWRANK
        row = jnp.where(j < LR0 // WPREP_COLS, j * WPREP_COLS, jnp.where(j < n_main, shifted, LR0))
        return pl.multiple_of(row, GLA_LOWRANK)

    return pl.pallas_call(
        _wprep_kernel,
        grid=(n_main + 1,),
        in_specs=[pl.BlockSpec((pl.Element(WPREP_COLS), pl.Element(d)), lambda j: (src_row(j), 0))],
        out_specs=pl.BlockSpec((d, WPREP_COLS), lambda j: (0, j)),
        out_shape=jax.ShapeDtypeStruct((d, (LR_BLK + 1) * LANE), jnp.bfloat16),
        compiler_params=pltpu.CompilerParams(
            dimension_semantics=("arbitrary",),
            vmem_limit_bytes=VMEM_LIMIT),
        name="in_weight_prep",
    )(w_in[0].T)


def _upper_runs(x, s):
    return jnp.concatenate([x[r:r + s] for r in range(s, BLK, 2 * s)], axis=0)


def _decay_factors(g, planes):
    row = lax.broadcasted_iota(jnp.int32, g.shape, 0)
    factors = {1: jnp.where((row & 1) != 0, g, 1.0)}
    for i, s in enumerate(PLANE_LEVELS):
        factors[s] = jnp.exp2(planes[i * BLK:(i + 1) * BLK])
    n = len(PLANE_LEVELS)
    groups = range(0, BLK, SUBLANE)
    pre = jnp.exp2(planes[n * BLK:(n + 1) * BLK])
    suf = jnp.exp2(planes[(n + 1) * BLK:(n + 2) * BLK])
    pre = [pre[r:r + SUBLANE] for r in groups]
    suf = [suf[r:r + SUBLANE] for r in groups]
    tot = [p * s for p, s in zip(pre, suf)]
    s = SUBLANE
    while s < BLK:
        factors[s] = jnp.concatenate([pre[v] if (r & s) else suf[v] for v, r in enumerate(groups)], axis=0)
        m = s // SUBLANE
        for v, r in enumerate(groups):
            run = v // m
            if r & s:
                pre[v] = pre[v] * tot[run - 1]
            else:
                suf[v] = suf[v] * tot[run + 1]
        tot = [tot[i] * tot[i + 1] for i in range(0, len(tot), 2)]
        s *= 2
    return factors, jnp.concatenate(pre, axis=0), jnp.concatenate(suf, axis=0), tot[0][0:1, :]


def _prep_block(q, k, g, planes):
    factors, pre, suf, total = _decay_factors(g, planes)
    row = lax.broadcasted_iota(jnp.int32, q.shape, 0)
    lhs, w = [], []
    for s in LEVELS:
        if s >= SUBLANE:
            qk = jnp.concatenate([(q if (r & s) else k)[r:r + SUBLANE] for r in range(0, BLK, SUBLANE)], axis=0)
            p32 = qk * factors[s]
            lhs.append(_upper_runs(p32, s).astype(jnp.bfloat16))
        else:
            p32 = jnp.where((row & s) != 0, q, k) * factors[s]
            lhs.append(None)
        w.append(p32.astype(jnp.bfloat16))
    return dict(
        lhs=lhs, w=w,
        q_in=(q * pre).astype(jnp.bfloat16),
        k_dec=(k * suf).astype(jnp.bfloat16),
        decay=total,
        dcol=jnp.sum(q * k, axis=-1, keepdims=True))


def _level_scores(ops):
    out = [[], []]
    for li in range(len(LEVELS)):
        w2 = jnp.concatenate([ops[0]['w'][li], ops[1]['w'][li]], axis=1)
        wt = lax.bitcast_convert_type(lax.bitcast_convert_type(w2, jnp.uint16).T, jnp.bfloat16)
        for hh in range(2):
            lhs = ops[hh]['lhs'][li]
            lhs = ops[hh]['w'][li] if lhs is None else lhs
            out[hh].append(jnp.dot(lhs, wt[hh * LANE:(hh + 1) * LANE], preferred_element_type=jnp.float32))
    return out


def _assemble_scores(g, dcol, lvl_ref):
    groups = []
    for r in range(0, BLK, SUBLANE):
        lvl = lvl_ref[r:r + SUBLANE, :]
        a = jnp.where(lvl == DIAG_LEVEL, dcol[r:r + SUBLANE], 0.0)
        for li, s in enumerate(LEVELS):
            if s >= SUBLANE:
                if r & s:
                    k = (r // (2 * s)) * s + (r % s)
                    a = jnp.where(lvl == li, g[li][k:k + SUBLANE], a)
            else:
                a = jnp.where(lvl == li, g[li][r:r + SUBLANE], a)
        groups.append(a)
    return jnp.concatenate(groups, axis=0).astype(jnp.bfloat16)


def _finish_block(a, ops, v_bf, st_ref, hh, fresh):
    st = st_ref[hh]
    if fresh is not None:
        st = jnp.where(fresh, 0.0, st)
    o = jnp.dot(a, v_bf, preferred_element_type=jnp.float32)
    o = o + jnp.dot(ops['q_in'], st.astype(jnp.bfloat16).T, preferred_element_type=jnp.float32)
    upd = jnp.dot(v_bf.T, ops['k_dec'], preferred_element_type=jnp.float32)
    st_ref[hh] = st * ops['decay'] + upd
    return o


def _head_out(o, gate, gain, dk):
    ms = jnp.mean(o * o, axis=-1, keepdims=True)
    y = o * lax.rsqrt(ms + EPS * dk) * gain
    return y * (gate / (1.0 + jnp.exp(-gate)))


def _sweep(items, fillers, mall_ref, lvl_ref):
    n = len(items)

    def exponents(i):
        logg, g, ks = items[i]['gates']()
        return jnp.dot(mall_ref[...], logg.astype(jnp.bfloat16), preferred_element_type=jnp.float32), g, ks

    def operands(i, planes_g_ks):
        planes, g, ks = planes_g_ks
        return [_prep_block(items[i]['q'](hh), ks[hh], g[:, hh * LANE:(hh + 1) * LANE],
                            planes[:, hh * LANE:(hh + 1) * LANE]) for hh in range(2)]

    planes = {0: exponents(0), 1: exponents(1)}
    ops = {0: operands(0, planes.pop(0))}
    issued = 0
    for i in range(n):
        g = _level_scores(ops[i])
        if i + 2 < n:
            planes[i + 2] = exponents(i + 2)
        while issued < ((i + 1) * len(fillers)) // n:
            fillers[issued]()
            issued += 1
        a = [_assemble_scores(g[hh], ops[i][hh]['dcol'], lvl_ref) for hh in range(2)]
        if i + 1 < n:
            ops[i + 1] = operands(i + 1, planes.pop(i + 1))
        cur = ops.pop(i)
        it = items[i]
        for hh in range(2):
            it['out'](hh, _finish_block(a[hh], cur[hh], it['v'](hh), it['st'], hh, it['fresh']))


def _gla_item(buf, p, c, fresh, wup_ref, bup_ref, gain_ref, st_ref, out_ref, out_row0):
    rows = slice(c * BLK, (c + 1) * BLK)
    pair = slice(p * 2 * LANE, (p + 1) * 2 * LANE)

    def gates():
        z = jnp.dot(buf[LR_BLK, rows, :], wup_ref[:, pair], preferred_element_type=jnp.float32) + bup_ref[:, pair]
        logg = (jnp.minimum(z, 0.0) * LOG2E - jnp.log2(1.0 + jnp.exp(-jnp.abs(z)))) * (1.0 / GLA_TAU)
        return logg, jnp.exp2(logg), [buf[GLA_K0 + 2 * p + hh, rows, :].astype(jnp.float32) for hh in range(2)]

    def wide(first, hh):
        blk = first + 4 * p + 2 * hh
        return jnp.concatenate([buf[blk, rows, :], buf[blk + 1, rows, :]], axis=-1)

    def out(hh, o):
        cols = slice((2 * p + hh) * GLA_HEAD_V, (2 * p + hh + 1) * GLA_HEAD_V)
        y = _head_out(o, wide(GLA_G0, hh).astype(jnp.float32), gain_ref[:, cols], GLA_HEAD_K)
        out_ref[out_row0 + c * BLK:out_row0 + (c + 1) * BLK, cols] = y.astype(out_ref.dtype)

    return dict(gates=gates, q=lambda hh: buf[GLA_Q0 + 2 * p + hh, rows, :].astype(jnp.float32),
                v=lambda hh: wide(GLA_V0, hh), out=out, st=st_ref.at[p], fresh=fresh if c == 0 else None)


def _hgrn_item(buf, p, c, fresh, lb, gain_ref, st_ref, out_ref, out_row0):
    rows = slice(c * BLK, (c + 1) * BLK)
    lb_p = lb[:, p * 2 * LANE:(p + 1) * 2 * LANE]

    def gates():
        hf = jnp.concatenate([buf[HG_F0 + 2 * p, rows, :], buf[HG_F0 + 2 * p + 1, rows, :]],
                             axis=-1).astype(jnp.float32)
        f = lb_p + (1.0 - lb_p) * (1.0 / (1.0 + jnp.exp(-hf)))
        kk = 1.0 - f
        return jnp.log2(f), f, [kk[:, hh * LANE:(hh + 1) * LANE] for hh in range(2)]

    def out(hh, o):
        cols = slice((2 * p + hh) * HGRN_HEAD_V, (2 * p + hh + 1) * HGRN_HEAD_V)
        y = _head_out(o, buf[HG_G0 + 2 * p + hh, rows, :].astype(jnp.float32), gain_ref[:, cols], HGRN_HEAD_K)
        out_ref[out_row0 + c * BLK:out_row0 + (c + 1) * BLK,
                GLA_WIDTH + cols.start:GLA_WIDTH + cols.stop] = y.astype(out_ref.dtype)

    return dict(gates=gates, q=lambda hh: buf[HG_Q0 + 2 * p + hh, rows, :].astype(jnp.float32),
                v=lambda hh: buf[HG_I0 + 2 * p + hh, rows, :], out=out, st=st_ref.at[p],
                fresh=fresh if c == 0 else None)


def _fused_kernel(tiles_per_seq, x_ref, nw_ref, w_ref, wup_ref, bup_ref, ggain_ref, lb_ref, hgain_ref, mall_ref,
                  lvl_ref, out_ref, buf_new, buf_cur, st_gla, st_hg):
    s = pl.program_id(0)

    @pl.when(s == 0)
    def _():
        buf_new[...] = jnp.zeros_like(buf_new)
        st_gla[...] = jnp.zeros_like(st_gla)
        st_hg[...] = jnp.zeros_like(st_hg)

    buf_cur[...] = buf_new[...]

    x = x_ref[...]
    ms = jnp.mean(x * x, axis=-1, keepdims=True)
    h = (x * lax.rsqrt(ms + EPS) * nw_ref[...]).astype(jnp.bfloat16)

    def project(j0):
        j1 = min(j0 + PROJ_CHUNK, w_ref.shape[1])
        acc = jnp.dot(h, w_ref[:, j0:j1], preferred_element_type=jnp.float32)
        for c in range((j1 - j0) // LANE):
            buf_new[j0 // LANE + c] = acc[:, c * LANE:(c + 1) * LANE].astype(buf_new.dtype)

    fillers = [functools.partial(project, j0) for j0 in range(0, w_ref.shape[1], PROJ_CHUNK)]
    for f in fillers[:LEAD_CHUNKS]:
        f()

    lg = lb_ref[...]
    e = jnp.exp(lg - jnp.max(lg, axis=0, keepdims=True))
    lb = e[0:1, :] / jnp.sum(e, axis=0, keepdims=True)

    fresh = lax.rem(s + tiles_per_seq - 1, tiles_per_seq) == 0
    items = []
    for c in range(TILE // BLK):
        for p in range(GLA_HEADS // 2):
            items.append(_gla_item(buf_cur, p, c, fresh, wup_ref, bup_ref, ggain_ref, st_gla, out_ref, 0))
        for p in range(HGRN_HEADS // 2):
            items.append(_hgrn_item(buf_cur, p, c, fresh, lb, hgain_ref, st_hg, out_ref, 0))
    _sweep(items, fillers[LEAD_CHUNKS:], mall_ref, lvl_ref)


def _fused_projection_recurrence(x2, norm_w, w_all, wup, bup, ggain, lb_logits, hgain, seq):
    bt = x2.shape[0]
    assert bt % TILE == 0 and seq % TILE == 0
    steps = bt // TILE + 1
    mall = jnp.asarray(_plane_matrix(), dtype=jnp.bfloat16)
    lvl = jnp.asarray(_level_matrix())
    const = lambda a: pl.BlockSpec(a.shape, lambda s: (0,) * a.ndim)
    return pl.pallas_call(
        functools.partial(_fused_kernel, seq // TILE),
        grid=(steps,),
        in_specs=[
            pl.BlockSpec((TILE, D_MODEL), lambda s: (jnp.minimum(s, steps - 2), 0)),
            const(norm_w),
            pl.BlockSpec(w_all.shape, lambda s: (0, 0), pipeline_mode=pl.Buffered(1)),
            const(wup), const(bup), const(ggain), const(lb_logits), const(hgain), const(mall), const(lvl),
        ],
        out_specs=pl.BlockSpec((TILE, GLA_WIDTH + HGRN_WIDTH), lambda s: (s, 0)),
        out_shape=jax.ShapeDtypeStruct((steps * TILE, GLA_WIDTH + HGRN_WIDTH), jnp.bfloat16),
        scratch_shapes=[
            pltpu.VMEM((LR_BLK + 1, TILE, LANE), jnp.bfloat16),
            pltpu.VMEM((LR_BLK + 1, TILE, LANE), jnp.bfloat16),
            pltpu.VMEM((GLA_HEADS // 2, 2, GLA_HEAD_V, LANE), jnp.float32),
            pltpu.VMEM((HGRN_HEADS // 2, 2, HGRN_HEAD_V, LANE), jnp.float32),
        ],
        compiler_params=pltpu.CompilerParams(
            dimension_semantics=("arbitrary",),
            vmem_limit_bytes=VMEM_LIMIT),
        name="in_proj_recurrence",
    )(x2, norm_w, w_all, wup, bup, ggain, lb_logits, hgain, mall, lvl)


def _out_kernel(m_ref, w_ref, x_ref, fw_ref, out_ref):
    for r0 in range(0, out_ref.shape[0], OUT_ROWS):
        rows = slice(r0, r0 + OUT_ROWS)
        y = jnp.dot(m_ref[rows, :], w_ref[...], preferred_element_type=jnp.float32) + x_ref[rows, :]
        ms = jnp.mean(y * y, axis=-1, keepdims=True)
        out_ref[rows, :] = y * lax.rsqrt(ms + EPS) * fw_ref[...]


def _output_projection(mixed, row0, wo, x2, final_w, tm=1024):
    bt = x2.shape[0]
    d_inner = mixed.shape[1]
    return pl.pallas_call(
        _out_kernel,
        grid=(bt // tm,),
        in_specs=[
            pl.BlockSpec((pl.Element(tm), pl.Element(d_inner)), lambda i: (pl.multiple_of(row0 + i * tm, TILE), 0)),
            pl.BlockSpec((d_inner, D_MODEL), lambda i: (0, 0), pipeline_mode=pl.Buffered(1)),
            pl.BlockSpec((tm, D_MODEL), lambda i: (i, 0)),
            pl.BlockSpec((1, D_MODEL), lambda i: (0, 0)),
        ],
        out_specs=pl.BlockSpec((tm, D_MODEL), lambda i: (i, 0)),
        out_shape=jax.ShapeDtypeStruct((bt, D_MODEL), jnp.float32),
        compiler_params=pltpu.CompilerParams(
            dimension_semantics=("arbitrary",),
            vmem_limit_bytes=VMEM_LIMIT),
        name="out_proj_residual_norm",
    )(mixed, wo, x2, final_w)


def kernel(x, norm_w, w_in, gla_w_up, gla_b_up, gla_norm_w, hgrn_lb_logits, hgrn_norm_w, w_out, final_norm_w):
    batch, seq, d = x.shape
    assert d == D_MODEL and norm_w.shape[0] == 1
    x2 = x.reshape(batch * seq, d)
    wup = jnp.pad(gla_w_up[0], ((0, LANE - GLA_LOWRANK), (0, 0))).astype(jnp.bfloat16)
    mixed = _fused_projection_recurrence(x2, norm_w, _prepare_in_weight(w_in), wup, gla_b_up, gla_norm_w,
                                         hgrn_lb_logits, hgrn_norm_w, seq)
    out = _output_projection(mixed, TILE, w_out[0].astype(jnp.bfloat16), x2, final_norm_w.reshape(1, d))
    return out.reshape(batch, seq, d)
```

```python
import functools

import numpy as np
import jax
import jax.numpy as jnp
from jax import lax
from jax.experimental import pallas as pl
from jax.experimental.pallas import tpu as pltpu

D_MODEL = 2048
GLA_HEADS = 4
GLA_HEAD_K = 128
GLA_HEAD_V = 256
GLA_KEY_WIDTH = GLA_HEADS * GLA_HEAD_K
GLA_WIDTH = GLA_HEADS * GLA_HEAD_V
GLA_LOWRANK = 16
GLA_TAU = 16.0
HGRN_HEADS = 8
HGRN_HEAD_K = 128
HGRN_HEAD_V = 128
HGRN_WIDTH = HGRN_HEADS * HGRN_HEAD_V
EPS = 1e-6

LANE = 128
SUBLANE = 8
BLK = 128
TILE = 2 * BLK
LEVELS = (64, 32, 16, 8, 4, 2, 1)
PLANE_LEVELS = (2, 4)
N_PLANES = len(PLANE_LEVELS) + 2
DIAG_LEVEL = len(LEVELS)
VMEM_LIMIT = 56 * 1024 * 1024
LOG2E = 1.4426950408889634

GLA_Q0, GLA_K0, GLA_V0, GLA_G0 = 0, 4, 8, 16
HG_Q0, HG_F0, HG_I0, HG_G0 = 24, 32, 40, 48
LR_BLK = 56
LR0 = GLA_G0 * LANE + GLA_WIDTH
PROJ_CHUNK = 512
WPREP_COLS = 512
LEAD_CHUNKS = 2
OUT_ROWS = 256


def _plane_matrix():
    r = np.arange(BLK)[:, None]
    t = np.arange(BLK)[None, :]
    planes = []
    for s in PLANE_LEVELS:
        start = (r // (2 * s)) * (2 * s)
        mid = start + s - 1
        upper = (r - start) >= s
        planes.append(np.where(upper, (t > mid) & (t <= r), (t > r) & (t <= mid)))
    start = (r // SUBLANE) * SUBLANE
    planes.append((t >= start) & (t <= r))
    planes.append((t > r) & (t < start + SUBLANE))
    return np.concatenate(planes, axis=0).astype(np.float32)


def _level_matrix():
    i = np.arange(BLK)[:, None]
    j = np.arange(BLK)[None, :]
    x = i ^ j
    msb = np.zeros_like(x)
    for bit in range(7):
        msb = np.where(x >> bit > 0, bit, msb)
    lvl = (len(LEVELS) - 1) - msb
    lvl = np.where(i == j, DIAG_LEVEL, lvl)
    lvl = np.where(i < j, -1, lvl)
    return lvl.astype(np.int32)


def _wprep_kernel(wt_ref, out_ref):
    j = pl.program_id(0)
    xt = wt_ref[...].astype(out_ref.dtype).T
    lane = lax.broadcasted_iota(jnp.int32, xt.shape, 1)
    keep = jnp.logical_or(j < LR_BLK * LANE // WPREP_COLS, lane < GLA_LOWRANK)
    out_ref[...] = jnp.where(keep, xt, jnp.zeros_like(xt))


def _prepare_in_weight(w_in):
    d, n_in = w_in.shape[1], w_in.shape[2]
    assert n_in == LR_BLK * LANE + GLA_LOWRANK and LR0 % WPREP_COLS == 0
    n_main = LR_BLK * LANE // WPREP_COLS

    def src_row(j):
        shifted = j * WPREP_COLS + GLA_LOWRANK
        row = jnp.where(j < LR0 // WPREP_COLS, j * WPREP_COLS, jnp.where(j < n_main, shifted, LR0))
        return pl.multiple_of(row, GLA_LOWRANK)

    return pl.pallas_call(
        _wprep_kernel,
        grid=(n_main + 1,),
        in_specs=[pl.BlockSpec((pl.Element(WPREP_COLS), pl.Element(d)), lambda j: (src_row(j), 0))],
        out_specs=pl.BlockSpec((d, WPREP_COLS), lambda j: (0, j)),
        out_shape=jax.ShapeDtypeStruct((d, (LR_BLK + 1) * LANE), jnp.bfloat16),
        compiler_params=pltpu.CompilerParams(
            dimension_semantics=("arbitrary",),
            vmem_limit_bytes=VMEM_LIMIT),
        name="in_weight_prep",
    )(w_in[0].T)


def _upper_runs(x, s):
    return jnp.concatenate([x[r:r + s] for r in range(s, BLK, 2 * s)], axis=0)


def _decay_factors(g, planes):
    row = lax.broadcasted_iota(jnp.int32, g.shape, 0)
    factors = {1: jnp.where((row & 1) != 0, g, 1.0)}
    for i, s in enumerate(PLANE_LEVELS):
        factors[s] = jnp.exp2(planes[i * BLK:(i + 1) * BLK])
    n = len(PLANE_LEVELS)
    groups = range(0, BLK, SUBLANE)
    pre = jnp.exp2(planes[n * BLK:(n + 1) * BLK])
    suf = jnp.exp2(planes[(n + 1) * BLK:(n + 2) * BLK])
    pre = [pre[r:r + SUBLANE] for r in groups]
    suf = [suf[r:r + SUBLANE] for r in groups]
    tot = [p * s for p, s in zip(pre, suf)]
    s = SUBLANE
    while s < BLK:
        factors[s] = jnp.concatenate([pre[v] if (r & s) else suf[v] for v, r in enumerate(groups)], axis=0)
        m = s // SUBLANE
        for v, r in enumerate(groups):
            run = v // m
            if r & s:
                pre[v] = pre[v] * tot[run - 1]
            else:
                suf[v] = suf[v] * tot[run + 1]
        tot = [tot[i] * tot[i + 1] for i in range(0, len(tot), 2)]
        s *= 2
    return factors, jnp.concatenate(pre, axis=0), jnp.concatenate(suf, axis=0), tot[0][0:1, :]


def _prep_block(q, k, g, planes):
    factors, pre, suf, total = _decay_factors(g, planes)
    row = lax.broadcasted_iota(jnp.int32, q.shape, 0)
    lhs, w = [], []
    for s in LEVELS:
        if s >= SUBLANE:
            qk = jnp.concatenate([(q if (r & s) else k)[r:r + SUBLANE] for r in range(0, BLK, SUBLANE)], axis=0)
            p32 = qk * factors[s]
            lhs.append(_upper_runs(p32, s).astype(jnp.bfloat16))
        else:
            p32 = jnp.where((row & s) != 0, q, k) * factors[s]
            lhs.append(None)
        w.append(p32.astype(jnp.bfloat16))
    return dict(
        lhs=lhs, w=w,
        q_in=(q * pre).astype(jnp.bfloat16),
        k_dec=(k * suf).astype(jnp.bfloat16),
        decay=total,
        dcol=jnp.sum(q * k, axis=-1, keepdims=True))


def _xlu_transpose(x):
    return lax.bitcast_convert_type(lax.bitcast_convert_type(x, jnp.uint16).T, jnp.bfloat16)


def _level_scores(ops):
    out = [[], []]
    for li in range(len(LEVELS)):
        wt = _xlu_transpose(jnp.concatenate([ops[0]['w'][li], ops[1]['w'][li]], axis=1))
        for hh in range(2):
            lhs = ops[hh]['lhs'][li]
            lhs = ops[hh]['w'][li] if lhs is None else lhs
            out[hh].append(jnp.dot(lhs, wt[hh * LANE:(hh + 1) * LANE], preferred_element_type=jnp.float32))
    return out


def _assemble_scores(g, dcol, lvl_ref):
    groups = []
    for r in range(0, BLK, SUBLANE):
        lvl = lvl_ref[r:r + SUBLANE, :]
        a = jnp.where(lvl == DIAG_LEVEL, dcol[r:r + SUBLANE], 0.0)
        for li, s in enumerate(LEVELS):
            if s >= SUBLANE:
                if r & s:
                    k = (r // (2 * s)) * s + (r % s)
                    a = jnp.where(lvl == li, g[li][k:k + SUBLANE], a)
            else:
                a = jnp.where(lvl == li, g[li][r:r + SUBLANE], a)
        groups.append(a)
    return jnp.concatenate(groups, axis=0).astype(jnp.bfloat16)


def _finish_block(a, ops, v_bf, st_ref, hh, fresh):
    st = st_ref[hh]
    if fresh is not None:
        st = jnp.where(fresh, 0.0, st)
    o = jnp.dot(a, v_bf, preferred_element_type=jnp.float32)
    o = o + jnp.dot(ops['q_in'], _xlu_transpose(st.astype(jnp.bfloat16)), preferred_element_type=jnp.float32)
    upd = jnp.dot(v_bf.T, ops['k_dec'], preferred_element_type=jnp.float32)
    st_ref[hh] = st * ops['decay'] + upd
    return o


def _head_out(o, gate, gain, dk):
    ms = jnp.mean(o * o, axis=-1, keepdims=True)
    y = o * lax.rsqrt(ms + EPS * dk) * gain
    return y * (gate / (1.0 + jnp.exp(-gate)))


def _sweep(items, fillers, mall_ref, lvl_ref):
    n = len(items)

    def exponents(i):
        logg, g, ks = items[i]['gates']()
        return jnp.dot(mall_ref[...], logg.astype(jnp.bfloat16), preferred_element_type=jnp.float32), g, ks

    def operands(i, planes_g_ks):
        planes, g, ks = planes_g_ks
        return [_prep_block(items[i]['q'](hh), ks[hh], g[:, hh * LANE:(hh + 1) * LANE],
                            planes[:, hh * LANE:(hh + 1) * LANE]) for hh in range(2)]

    planes = {0: exponents(0), 1: exponents(1)}
    ops = {0: operands(0, planes.pop(0))}
    issued = 0
    for i in range(n):
        g = _level_scores(ops[i])
        if i + 2 < n:
            planes[i + 2] = exponents(i + 2)
        while issued < ((i + 1) * len(fillers)) // n:
            fillers[issued]()
            issued += 1
        a = [_assemble_scores(g[hh], ops[i][hh]['dcol'], lvl_ref) for hh in range(2)]
        if i + 1 < n:
            ops[i + 1] = operands(i + 1, planes.pop(i + 1))
        cur = ops.pop(i)
        it = items[i]
        for hh in range(2):
            it['out'](hh, _finish_block(a[hh], cur[hh], it['v'](hh), it['st'], hh, it['fresh']))


def _gla_item(buf, p, c, fresh, wup_ref, bup_ref, gain_ref, st_ref, out_ref, out_row0):
    rows = slice(c * BLK, (c + 1) * BLK)
    pair = slice(p * 2 * LANE, (p + 1) * 2 * LANE)

    def gates():
        z = jnp.dot(buf[LR_BLK, rows, :], wup_ref[:, pair], preferred_element_type=jnp.float32) + bup_ref[:, pair]
        logg = (jnp.minimum(z, 0.0) * LOG2E - jnp.log2(1.0 + jnp.exp(-jnp.abs(z)))) * (1.0 / GLA_TAU)
        return logg, jnp.exp2(logg), [buf[GLA_K0 + 2 * p + hh, rows, :].astype(jnp.float32) for hh in range(2)]

    def wide(first, hh):
        blk = first + 4 * p + 2 * hh
        return jnp.concatenate([buf[blk, rows, :], buf[blk + 1, rows, :]], axis=-1)

    def out(hh, o):
        cols = slice((2 * p + hh) * GLA_HEAD_V, (2 * p + hh + 1) * GLA_HEAD_V)
        y = _head_out(o, wide(GLA_G0, hh).astype(jnp.float32), gain_ref[:, cols], GLA_HEAD_K)
        out_ref[out_row0 + c * BLK:out_row0 + (c + 1) * BLK, cols] = y.astype(out_ref.dtype)

    return dict(gates=gates, q=lambda hh: buf[GLA_Q0 + 2 * p + hh, rows, :].astype(jnp.float32),
                v=lambda hh: wide(GLA_V0, hh), out=out, st=st_ref.at[p], fresh=fresh if c == 0 else None)


def _hgrn_item(buf, p, c, fresh, lb, gain_ref, st_ref, out_ref, out_row0):
    rows = slice(c * BLK, (c + 1) * BLK)
    lb_p = lb[:, p * 2 * LANE:(p + 1) * 2 * LANE]

    def gates():
        hf = jnp.concatenate([buf[HG_F0 + 2 * p, rows, :], buf[HG_F0 + 2 * p + 1, rows, :]],
                             axis=-1).astype(jnp.float32)
        f = lb_p + (1.0 - lb_p) * (1.0 / (1.0 + jnp.exp(-hf)))
        kk = 1.0 - f
        return jnp.log2(f), f, [kk[:, hh * LANE:(hh + 1) * LANE] for hh in range(2)]

    def out(hh, o):
        cols = slice((2 * p + hh) * HGRN_HEAD_V, (2 * p + hh + 1) * HGRN_HEAD_V)
        y = _head_out(o, buf[HG_G0 + 2 * p + hh, rows, :].astype(jnp.float32), gain_ref[:, cols], HGRN_HEAD_K)
        out_ref[out_row0 + c * BLK:out_row0 + (c + 1) * BLK,
                GLA_WIDTH + cols.start:GLA_WIDTH + cols.stop] = y.astype(out_ref.dtype)

    return dict(gates=gates, q=lambda hh: buf[HG_Q0 + 2 * p + hh, rows, :].astype(jnp.float32),
                v=lambda hh: buf[HG_I0 + 2 * p + hh, rows, :], out=out, st=st_ref.at[p],
                fresh=fresh if c == 0 else None)


def _fused_kernel(tiles_per_seq, x0_ref, xn_ref, nw_ref, w_ref, wup_ref, bup_ref, ggain_ref, lb_ref, hgain_ref,
                  mall_ref, lvl_ref, out_ref, buf_new, buf_cur, h_ref, st_gla, st_hg):
    s = pl.program_id(0)

    def normed(x_ref):
        x = x_ref[...]
        ms = jnp.mean(x * x, axis=-1, keepdims=True)
        return (x * lax.rsqrt(ms + EPS) * nw_ref[...]).astype(h_ref.dtype)

    @pl.when(s == 0)
    def _():
        buf_new[...] = jnp.zeros_like(buf_new)
        st_gla[...] = jnp.zeros_like(st_gla)
        st_hg[...] = jnp.zeros_like(st_hg)
        h_ref[...] = normed(x0_ref)

    buf_cur[...] = buf_new[...]

    def project(j0):
        j1 = min(j0 + PROJ_CHUNK, w_ref.shape[1])
        acc = jnp.dot(h_ref[...], w_ref[:, j0:j1], preferred_element_type=jnp.float32)
        for c in range((j1 - j0) // LANE):
            buf_new[j0 // LANE + c] = acc[:, c * LANE:(c + 1) * LANE].astype(buf_new.dtype)

    fillers = [functools.partial(project, j0) for j0 in range(0, w_ref.shape[1], PROJ_CHUNK)]
    for f in fillers[:LEAD_CHUNKS]:
        f()

    lg = lb_ref[...]
    e = jnp.exp(lg - jnp.max(lg, axis=0, keepdims=True))
    lb = e[0:1, :] / jnp.sum(e, axis=0, keepdims=True)

    fresh = lax.rem(s + tiles_per_seq - 1, tiles_per_seq) == 0
    items = []
    for c in range(TILE // BLK):
        for p in range(GLA_HEADS // 2):
            items.append(_gla_item(buf_cur, p, c, fresh, wup_ref, bup_ref, ggain_ref, st_gla, out_ref, 0))
        for p in range(HGRN_HEADS // 2):
            items.append(_hgrn_item(buf_cur, p, c, fresh, lb, hgain_ref, st_hg, out_ref, 0))
    _sweep(items, fillers[LEAD_CHUNKS:], mall_ref, lvl_ref)
    h_ref[...] = normed(xn_ref)


def _fused_projection_recurrence(x2, norm_w, w_all, wup, bup, ggain, lb_logits, hgain, seq):
    bt = x2.shape[0]
    assert bt % TILE == 0 and seq % TILE == 0
    steps = bt // TILE + 1
    mall = jnp.asarray(_plane_matrix(), dtype=jnp.bfloat16)
    lvl = jnp.asarray(_level_matrix())
    const = lambda a: pl.BlockSpec(a.shape, lambda s: (0,) * a.ndim)
    return pl.pallas_call(
        functools.partial(_fused_kernel, seq // TILE),
        grid=(steps,),
        in_specs=[
            pl.BlockSpec((TILE, D_MODEL), lambda s: (0, 0)),
            pl.BlockSpec((TILE, D_MODEL), lambda s: (jnp.minimum(s + 1, steps - 2), 0)),
            const(norm_w),
            pl.BlockSpec(w_all.shape, lambda s: (0, 0), pipeline_mode=pl.Buffered(1)),
            const(wup), const(bup), const(ggain), const(lb_logits), const(hgain), const(mall), const(lvl),
        ],
        out_specs=pl.BlockSpec((TILE, GLA_WIDTH + HGRN_WIDTH), lambda s: (s, 0)),
        out_shape=jax.ShapeDtypeStruct((steps * TILE, GLA_WIDTH + HGRN_WIDTH), jnp.bfloat16),
        scratch_shapes=[
            pltpu.VMEM((LR_BLK + 1, TILE, LANE), jnp.bfloat16),
            pltpu.VMEM((LR_BLK + 1, TILE, LANE), jnp.bfloat16),
            pltpu.VMEM((TILE, D_MODEL), jnp.bfloat16),
            pltpu.VMEM((GLA_HEADS // 2, 2, GLA_HEAD_V, LANE), jnp.float32),
            pltpu.VMEM((HGRN_HEADS // 2, 2, HGRN_HEAD_V, LANE), jnp.float32),
        ],
        compiler_params=pltpu.CompilerParams(
            dimension_semantics=("arbitrary",),
            vmem_limit_bytes=VMEM_LIMIT),
        name="in_proj_recurrence",
    )(x2, x2, norm_w, w_all, wup, bup, ggain, lb_logits, hgain, mall, lvl)


def _out_kernel(m_ref, w_ref, x_ref, fw_ref, out_ref):
    for r0 in range(0, out_ref.shape[0], OUT_ROWS):
        rows = slice(r0, r0 + OUT_ROWS)
        y = jnp.dot(m_ref[rows, :], w_ref[...], preferred_element_type=jnp.float32) + x_ref[rows, :]
        ms = jnp.mean(y * y, axis=-1, keepdims=True)
        out_ref[rows, :] = y * lax.rsqrt(ms + EPS) * fw_ref[...]


def _output_projection(mixed, row0, wo, x2, final_w, tm=512):
    bt = x2.shape[0]
    d_inner = mixed.shape[1]
    return pl.pallas_call(
        _out_kernel,
        grid=(bt // tm,),
        in_specs=[
            pl.BlockSpec((pl.Element(tm), pl.Element(d_inner)), lambda i: (pl.multiple_of(row0 + i * tm, TILE), 0)),
            pl.BlockSpec((d_inner, D_MODEL), lambda i: (0, 0), pipeline_mode=pl.Buffered(1)),
            pl.BlockSpec((tm, D_MODEL), lambda i: (i, 0)),
            pl.BlockSpec((1, D_MODEL), lambda i: (0, 0)),
        ],
        out_specs=pl.BlockSpec((tm, D_MODEL), lambda i: (i, 0)),
        out_shape=jax.ShapeDtypeStruct((bt, D_MODEL), jnp.float32),
        compiler_params=pltpu.CompilerParams(
            dimension_semantics=("arbitrary",),
            vmem_limit_bytes=VMEM_LIMIT),
        name="out_proj_residual_norm",
    )(mixed, wo, x2, final_w)


def kernel(x, norm_w, w_in, gla_w_up, gla_b_up, gla_norm_w, hgrn_lb_logits, hgrn_norm_w, w_out, final_norm_w):
    batch, seq, d = x.shape
    assert d == D_MODEL and norm_w.shape[0] == 1
    x2 = x.reshape(batch * seq, d)
    wup = jnp.pad(gla_w_up[0], ((0, LANE - GLA_LOWRANK), (0, 0))).astype(jnp.bfloat16)
    mixed = _fused_projection_recurrence(x2, norm_w, _prepare_in_weight(w_in), wup, gla_b_up, gla_norm_w,
                                         hgrn_lb_logits, hgrn_norm_w, seq)
    out = _output_projection(mixed, TILE, w_out[0].astype(jnp.bfloat16), x2, final_norm_w.reshape(1, d))
    return out.reshape(batch, seq, d)
```

```python
import functools

import numpy as np
import jax
import jax.numpy as jnp
from jax import lax
from jax.experimental import pallas as pl
from jax.experimental.pallas import tpu as pltpu

D_MODEL = 2048
GLA_HEADS = 4
GLA_HEAD_K = 128
GLA_HEAD_V = 256
GLA_KEY_WIDTH = GLA_HEADS * GLA_HEAD_K
GLA_WIDTH = GLA_HEADS * GLA_HEAD_V
GLA_LOWRANK = 16
GLA_TAU = 16.0
HGRN_HEADS = 8
HGRN_HEAD_K = 128
HGRN_HEAD_V = 128
HGRN_WIDTH = HGRN_HEADS * HGRN_HEAD_V
EPS = 1e-6

LANE = 128
SUBLANE = 8
BLK = 128
TILE = 2 * BLK
LEVELS = (64, 32, 16, 8, 4, 2, 1)
MXU_LEVELS = LEVELS[:-1]
PLANE_LEVELS = (2, 4)
N_PLANES = len(PLANE_LEVELS) + 2
DIAG_LEVEL = len(LEVELS)
VMEM_LIMIT = 56 * 1024 * 1024
LOG2E = 1.4426950408889634

GLA_Q0, GLA_K0, GLA_V0, GLA_G0 = 0, 4, 8, 16
HG_Q0, HG_F0, HG_I0, HG_G0 = 24, 32, 40, 48
LR_BLK = 56
LR0 = GLA_G0 * LANE + GLA_WIDTH
PROJ_CHUNK = 512
WPREP_COLS = 512
LEAD_CHUNKS = 2
OUT_ROWS = 256


def _plane_matrix():
    r = np.arange(BLK)[:, None]
    t = np.arange(BLK)[None, :]
    planes = []
    for s in PLANE_LEVELS:
        start = (r // (2 * s)) * (2 * s)
        mid = start + s - 1
        upper = (r - start) >= s
        planes.append(np.where(upper, (t > mid) & (t <= r), (t > r) & (t <= mid)))
    start = (r // SUBLANE) * SUBLANE
    planes.append((t >= start) & (t <= r))
    planes.append((t > r) & (t < start + SUBLANE))
    return np.concatenate(planes, axis=0).astype(np.float32)


def _level_matrix():
    i = np.arange(BLK)[:, None]
    j = np.arange(BLK)[None, :]
    x = i ^ j
    msb = np.zeros_like(x)
    for bit in range(7):
        msb = np.where(x >> bit > 0, bit, msb)
    lvl = (len(LEVELS) - 1) - msb
    lvl = np.where(i == j, DIAG_LEVEL, lvl)
    lvl = np.where(i < j, -1, lvl)
    return lvl.astype(np.int32)


def _wprep_kernel(wt_ref, out_ref):
    j = pl.program_id(0)
    xt = wt_ref[...].astype(out_ref.dtype).T
    lane = lax.broadcasted_iota(jnp.int32, xt.shape, 1)
    keep = jnp.logical_or(j < LR_BLK * LANE // WPREP_COLS, lane < GLA_LOWRANK)
    out_ref[...] = jnp.where(keep, xt, jnp.zeros_like(xt))


def _prepare_in_weight(w_in):
    d, n_in = w_in.shape[1], w_in.shape[2]
    assert n_in == LR_BLK * LANE + GLA_LOWRANK and LR0 % WPREP_COLS == 0
    n_main = LR_BLK * LANE // WPREP_COLS

    def src_row(j):
        shifted = j * WPREP_COLS + GLA_LOWRANK
        row = jnp.where(j < LR0 // WPREP_COLS, j * WPREP_COLS, jnp.where(j < n_main, shifted, LR0))
        return pl.multiple_of(row, GLA_LOWRANK)

    return pl.pallas_call(
        _wprep_kernel,
        grid=(n_main + 1,),
        in_specs=[pl.BlockSpec((pl.Element(WPREP_COLS), pl.Element(d)), lambda j: (src_row(j), 0))],
        out_specs=pl.BlockSpec((d, WPREP_COLS), lambda j: (0, j)),
        out_shape=jax.ShapeDtypeStruct((d, (LR_BLK + 1) * LANE), jnp.bfloat16),
        compiler_params=pltpu.CompilerParams(
            dimension_semantics=("arbitrary",),
            vmem_limit_bytes=VMEM_LIMIT),
        name="in_weight_prep",
    )(w_in[0].T)


def _upper_runs(x, s):
    return jnp.concatenate([x[r:r + s] for r in range(s, BLK, 2 * s)], axis=0)


def _decay_factors(g, planes):
    factors = {s: jnp.exp2(planes[i * BLK:(i + 1) * BLK]) for i, s in enumerate(PLANE_LEVELS)}
    n = len(PLANE_LEVELS)
    groups = range(0, BLK, SUBLANE)
    pre = jnp.exp2(planes[n * BLK:(n + 1) * BLK])
    suf = jnp.exp2(planes[(n + 1) * BLK:(n + 2) * BLK])
    pre = [pre[r:r + SUBLANE] for r in groups]
    suf = [suf[r:r + SUBLANE] for r in groups]
    tot = [p * s for p, s in zip(pre, suf)]
    s = SUBLANE
    while s < BLK:
        factors[s] = jnp.concatenate([pre[v] if (r & s) else suf[v] for v, r in enumerate(groups)], axis=0)
        m = s // SUBLANE
        for v, r in enumerate(groups):
            run = v // m
            if r & s:
                pre[v] = pre[v] * tot[run - 1]
            else:
                suf[v] = suf[v] * tot[run + 1]
        tot = [tot[i] * tot[i + 1] for i in range(0, len(tot), 2)]
        s *= 2
    return factors, jnp.concatenate(pre, axis=0), jnp.concatenate(suf, axis=0), tot[0][0:1, :]


def _prep_block(q, k, g, planes):
    factors, pre, suf, total = _decay_factors(g, planes)
    row = lax.broadcasted_iota(jnp.int32, q.shape, 0)
    k_prev = jnp.concatenate([pltpu.roll(k[r:r + SUBLANE], 1, axis=0) for r in range(0, BLK, SUBLANE)], axis=0)
    lhs, w = [], []
    for s in MXU_LEVELS:
        if s >= SUBLANE:
            qk = jnp.concatenate([(q if (r & s) else k)[r:r + SUBLANE] for r in range(0, BLK, SUBLANE)], axis=0)
            p32 = qk * factors[s]
            lhs.append(_upper_runs(p32, s).astype(jnp.bfloat16))
        else:
            p32 = jnp.where((row & s) != 0, q, k) * factors[s]
            lhs.append(None)
        w.append(p32.astype(jnp.bfloat16))
    return dict(
        lhs=lhs, w=w,
        q_in=(q * pre).astype(jnp.bfloat16),
        k_dec=(k * suf).astype(jnp.bfloat16),
        decay=total,
        dcol=jnp.sum(q * k, axis=-1, keepdims=True),
        dsub=jnp.sum(q * k_prev * g, axis=-1, keepdims=True))


def _xlu_transpose(x):
    return lax.bitcast_convert_type(lax.bitcast_convert_type(x, jnp.uint16).T, jnp.bfloat16)


def _level_scores(ops):
    out = [[], []]
    for li in range(len(MXU_LEVELS)):
        wt = _xlu_transpose(jnp.concatenate([ops[0]['w'][li], ops[1]['w'][li]], axis=1))
        for hh in range(2):
            lhs = ops[hh]['lhs'][li]
            lhs = ops[hh]['w'][li] if lhs is None else lhs
            out[hh].append(jnp.dot(lhs, wt[hh * LANE:(hh + 1) * LANE], preferred_element_type=jnp.float32))
    return out


def _assemble_scores(g, dcol, dsub, lvl_ref):
    groups = []
    for r in range(0, BLK, SUBLANE):
        lvl = lvl_ref[r:r + SUBLANE, :]
        a = jnp.where(lvl == DIAG_LEVEL, dcol[r:r + SUBLANE], 0.0)
        a = jnp.where(lvl == LEVELS.index(1), dsub[r:r + SUBLANE], a)
        for li, s in enumerate(MXU_LEVELS):
            if s >= SUBLANE:
                if r & s:
                    k = (r // (2 * s)) * s + (r % s)
                    a = jnp.where(lvl == li, g[li][k:k + SUBLANE], a)
            else:
                a = jnp.where(lvl == li, g[li][r:r + SUBLANE], a)
        groups.append(a)
    return jnp.concatenate(groups, axis=0).astype(jnp.bfloat16)


def _finish_block(a, ops, v_bf, st_ref, hh, fresh):
    st = st_ref[hh]
    if fresh is not None:
        st = jnp.where(fresh, 0.0, st)
    o = jnp.dot(a, v_bf, preferred_element_type=jnp.float32)
    o = o + jnp.dot(ops['q_in'], _xlu_transpose(st.astype(jnp.bfloat16)), preferred_element_type=jnp.float32)
    upd = jnp.dot(v_bf.T, ops['k_dec'], preferred_element_type=jnp.float32)
    st_ref[hh] = st * ops['decay'] + upd
    return o


def _head_out(o, gate, gain, dk):
    ms = jnp.mean(o * o, axis=-1, keepdims=True)
    y = o * lax.rsqrt(ms + EPS * dk) * gain
    return y * (gate / (1.0 + jnp.exp(-gate)))


def _sweep(items, fillers, mall_ref, lvl_ref):
    n = len(items)

    def exponents(i):
        logg, g, ks = items[i]['gates']()
        return jnp.dot(mall_ref[...], logg.astype(jnp.bfloat16), preferred_element_type=jnp.float32), g, ks

    def operands(i, planes_g_ks):
        planes, g, ks = planes_g_ks
        return [_prep_block(items[i]['q'](hh), ks[hh], g[:, hh * LANE:(hh + 1) * LANE],
                            planes[:, hh * LANE:(hh + 1) * LANE]) for hh in range(2)]

    planes = {0: exponents(0), 1: exponents(1)}
    ops = {0: operands(0, planes.pop(0))}
    issued = 0
    for i in range(n):
        g = _level_scores(ops[i])
        if i + 2 < n:
            planes[i + 2] = exponents(i + 2)
        while issued < ((i + 1) * len(fillers)) // n:
            fillers[issued]()
            issued += 1
        a = [_assemble_scores(g[hh], ops[i][hh]['dcol'], ops[i][hh]['dsub'], lvl_ref) for hh in range(2)]
        if i + 1 < n:
            ops[i + 1] = operands(i + 1, planes.pop(i + 1))
        cur = ops.pop(i)
        it = items[i]
        for hh in range(2):
            it['out'](hh, _finish_block(a[hh], cur[hh], it['v'](hh), it['st'], hh, it['fresh']))


def _gla_item(buf, p, c, fresh, wup_ref, bup_ref, gain_ref, st_ref, out_ref, out_row0):
    rows = slice(c * BLK, (c + 1) * BLK)
    pair = slice(p * 2 * LANE, (p + 1) * 2 * LANE)

    def gates():
        z = jnp.dot(buf[LR_BLK, rows, :], wup_ref[:, pair], preferred_element_type=jnp.float32) + bup_ref[:, pair]
        logg = (jnp.minimum(z, 0.0) * LOG2E - jnp.log2(1.0 + jnp.exp(-jnp.abs(z)))) * (1.0 / GLA_TAU)
        return logg, jnp.exp2(logg), [buf[GLA_K0 + 2 * p + hh, rows, :].astype(jnp.float32) for hh in range(2)]

    def wide(first, hh):
        blk = first + 4 * p + 2 * hh
        return jnp.concatenate([buf[blk, rows, :], buf[blk + 1, rows, :]], axis=-1)

    def out(hh, o):
        cols = slice((2 * p + hh) * GLA_HEAD_V, (2 * p + hh + 1) * GLA_HEAD_V)
        y = _head_out(o, wide(GLA_G0, hh).astype(jnp.float32), gain_ref[:, cols], GLA_HEAD_K)
        out_ref[out_row0 + c * BLK:out_row0 + (c + 1) * BLK, cols] = y.astype(out_ref.dtype)

    return dict(gates=gates, q=lambda hh: buf[GLA_Q0 + 2 * p + hh, rows, :].astype(jnp.float32),
                v=lambda hh: wide(GLA_V0, hh), out=out, st=st_ref.at[p], fresh=fresh if c == 0 else None)


def _hgrn_item(buf, p, c, fresh, lb, gain_ref, st_ref, out_ref, out_row0):
    rows = slice(c * BLK, (c + 1) * BLK)
    lb_p = lb[:, p * 2 * LANE:(p + 1) * 2 * LANE]

    def gates():
        hf = jnp.concatenate([buf[HG_F0 + 2 * p, rows, :], buf[HG_F0 + 2 * p + 1, rows, :]],
                             axis=-1).astype(jnp.float32)
        f = lb_p + (1.0 - lb_p) * (1.0 / (1.0 + jnp.exp(-hf)))
        kk = 1.0 - f
        return jnp.log2(f), f, [kk[:, hh * LANE:(hh + 1) * LANE] for hh in range(2)]

    def out(hh, o):
        cols = slice((2 * p + hh) * HGRN_HEAD_V, (2 * p + hh + 1) * HGRN_HEAD_V)
        y = _head_out(o, buf[HG_G0 + 2 * p + hh, rows, :].astype(jnp.float32), gain_ref[:, cols], HGRN_HEAD_K)
        out_ref[out_row0 + c * BLK:out_row0 + (c + 1) * BLK,
                GLA_WIDTH + cols.start:GLA_WIDTH + cols.stop] = y.astype(out_ref.dtype)

    return dict(gates=gates, q=lambda hh: buf[HG_Q0 + 2 * p + hh, rows, :].astype(jnp.float32),
                v=lambda hh: buf[HG_I0 + 2 * p + hh, rows, :], out=out, st=st_ref.at[p],
                fresh=fresh if c == 0 else None)


def _fused_kernel(tiles_per_seq, x0_ref, xn_ref, nw_ref, w_ref, wup_ref, bup_ref, ggain_ref, lb_ref, hgain_ref,
                  mall_ref, lvl_ref, out_ref, buf_new, buf_cur, h_ref, st_gla, st_hg):
    s = pl.program_id(0)

    def normed(x_ref):
        x = x_ref[...]
        ms = jnp.mean(x * x, axis=-1, keepdims=True)
        return (x * lax.rsqrt(ms + EPS) * nw_ref[...]).astype(h_ref.dtype)

    @pl.when(s == 0)
    def _():
        buf_new[...] = jnp.zeros_like(buf_new)
        st_gla[...] = jnp.zeros_like(st_gla)
        st_hg[...] = jnp.zeros_like(st_hg)
        h_ref[...] = normed(x0_ref)

    buf_cur[...] = buf_new[...]

    def project(j0):
        j1 = min(j0 + PROJ_CHUNK, w_ref.shape[1])
        acc = jnp.dot(h_ref[...], w_ref[:, j0:j1], preferred_element_type=jnp.float32)
        for c in range((j1 - j0) // LANE):
            buf_new[j0 // LANE + c] = acc[:, c * LANE:(c + 1) * LANE].astype(buf_new.dtype)

    fillers = [functools.partial(project, j0) for j0 in range(0, w_ref.shape[1], PROJ_CHUNK)]
    for f in fillers[:LEAD_CHUNKS]:
        f()

    lg = lb_ref[...]
    e = jnp.exp(lg - jnp.max(lg, axis=0, keepdims=True))
    lb = e[0:1, :] / jnp.sum(e, axis=0, keepdims=True)

    fresh = lax.rem(s + tiles_per_seq - 1, tiles_per_seq) == 0
    items = []
    for c in range(TILE // BLK):
        for p in range(GLA_HEADS // 2):
            items.append(_gla_item(buf_cur, p, c, fresh, wup_ref, bup_ref, ggain_ref, st_gla, out_ref, 0))
        for p in range(HGRN_HEADS // 2):
            items.append(_hgrn_item(buf_cur, p, c, fresh, lb, hgain_ref, st_hg, out_ref, 0))
    _sweep(items, fillers[LEAD_CHUNKS:], mall_ref, lvl_ref)
    h_ref[...] = normed(xn_ref)


def _fused_projection_recurrence(x2, norm_w, w_all, wup, bup, ggain, lb_logits, hgain, seq):
    bt = x2.shape[0]
    assert bt % TILE == 0 and seq % TILE == 0
    steps = bt // TILE + 1
    mall = jnp.asarray(_plane_matrix(), dtype=jnp.bfloat16)
    lvl = jnp.asarray(_level_matrix())
    const = lambda a: pl.BlockSpec(a.shape, lambda s: (0,) * a.ndim)
    return pl.pallas_call(
        functools.partial(_fused_kernel, seq // TILE),
        grid=(steps,),
        in_specs=[
            pl.BlockSpec((TILE, D_MODEL), lambda s: (0, 0)),
            pl.BlockSpec((TILE, D_MODEL), lambda s: (jnp.minimum(s + 1, steps - 2), 0)),
            const(norm_w),
            pl.BlockSpec(w_all.shape, lambda s: (0, 0), pipeline_mode=pl.Buffered(1)),
            const(wup), const(bup), const(ggain), const(lb_logits), const(hgain), const(mall), const(lvl),
        ],
        out_specs=pl.BlockSpec((TILE, GLA_WIDTH + HGRN_WIDTH), lambda s: (s, 0)),
        out_shape=jax.ShapeDtypeStruct((steps * TILE, GLA_WIDTH + HGRN_WIDTH), jnp.bfloat16),
        scratch_shapes=[
            pltpu.VMEM((LR_BLK + 1, TILE, LANE), jnp.bfloat16),
            pltpu.VMEM((LR_BLK + 1, TILE, LANE), jnp.bfloat16),
            pltpu.VMEM((TILE, D_MODEL), jnp.bfloat16),
            pltpu.VMEM((GLA_HEADS // 2, 2, GLA_HEAD_V, LANE), jnp.float32),
            pltpu.VMEM((HGRN_HEADS // 2, 2, HGRN_HEAD_V, LANE), jnp.float32),
        ],
        compiler_params=pltpu.CompilerParams(
            dimension_semantics=("arbitrary",),
            vmem_limit_bytes=VMEM_LIMIT),
        name="in_proj_recurrence",
    )(x2, x2, norm_w, w_all, wup, bup, ggain, lb_logits, hgain, mall, lvl)


def _out_kernel(m_ref, w_ref, x_ref, fw_ref, out_ref, wb_ref):
    @pl.when(pl.program_id(0) == 0)
    def _():
        wb_ref[...] = w_ref[0].astype(wb_ref.dtype)

    for r0 in range(0, out_ref.shape[0], OUT_ROWS):
        rows = slice(r0, r0 + OUT_ROWS)
        y = jnp.dot(m_ref[rows, :], wb_ref[...], preferred_element_type=jnp.float32) + x_ref[rows, :]
        ms = jnp.mean(y * y, axis=-1, keepdims=True)
        out_ref[rows, :] = y * lax.rsqrt(ms + EPS) * fw_ref[...]


def _output_projection(mixed, row0, wo, x2, final_w, tm=512):
    bt = x2.shape[0]
    d_inner = mixed.shape[1]
    return pl.pallas_call(
        _out_kernel,
        grid=(bt // tm,),
        in_specs=[
            pl.BlockSpec((pl.Element(tm), pl.Element(d_inner)), lambda i: (pl.multiple_of(row0 + i * tm, TILE), 0)),
            pl.BlockSpec((1, d_inner, D_MODEL), lambda i: (0, 0, 0), pipeline_mode=pl.Buffered(1)),
            pl.BlockSpec((tm, D_MODEL), lambda i: (i, 0)),
            pl.BlockSpec((1, D_MODEL), lambda i: (0, 0)),
        ],
        out_specs=pl.BlockSpec((tm, D_MODEL), lambda i: (i, 0)),
        out_shape=jax.ShapeDtypeStruct((bt, D_MODEL), jnp.float32),
        scratch_shapes=[pltpu.VMEM((d_inner, D_MODEL), jnp.bfloat16)],
        compiler_params=pltpu.CompilerParams(
            dimension_semantics=("arbitrary",),
            vmem_limit_bytes=VMEM_LIMIT),
        name="out_proj_residual_norm",
    )(mixed, wo, x2, final_w)


def kernel(x, norm_w, w_in, gla_w_up, gla_b_up, gla_norm_w, hgrn_lb_logits, hgrn_norm_w, w_out, final_norm_w):
    batch, seq, d = x.shape
    assert d == D_MODEL and norm_w.shape[0] == 1
    x2 = x.reshape(batch * seq, d)
    wup = jnp.pad(gla_w_up[0], ((0, LANE - GLA_LOWRANK), (0, 0))).astype(jnp.bfloat16)
    mixed = _fused_projection_recurrence(x2, norm_w, _prepare_in_weight(w_in), wup, gla_b_up, gla_norm_w,
                                         hgrn_lb_logits, hgrn_norm_w, seq)
    out = _output_projection(mixed, TILE, w_out, x2, final_norm_w.reshape(1, d))
    return out.reshape(batch, seq, d)
```

```python
import functools

import numpy as np
import jax
import jax.numpy as jnp
from jax import lax
from jax.experimental import pallas as pl
from jax.experimental.pallas import tpu as pltpu

D_MODEL = 2048
GLA_HEADS = 4
GLA_HEAD_K = 128
GLA_HEAD_V = 256
GLA_KEY_WIDTH = GLA_HEADS * GLA_HEAD_K
GLA_WIDTH = GLA_HEADS * GLA_HEAD_V
GLA_LOWRANK = 16
GLA_TAU = 16.0
HGRN_HEADS = 8
HGRN_HEAD_K = 128
HGRN_HEAD_V = 128
HGRN_WIDTH = HGRN_HEADS * HGRN_HEAD_V
EPS = 1e-6

LANE = 128
SUBLANE = 8
BLK = 128
TILE = 2 * BLK
LEVELS = (64, 32, 16, 8, 4, 2, 1)
NEAR_ROWS = 8
MXU_LEVELS = tuple(s for s in LEVELS if s >= NEAR_ROWS)
PLANE_LEVELS = tuple(s for s in (4,) if s >= NEAR_ROWS)
N_PLANES = len(PLANE_LEVELS) + 2
DIAG_LEVEL = len(LEVELS)
NEAR_CODE = 16
VMEM_LIMIT = 56 * 1024 * 1024
LOG2E = 1.4426950408889634

GLA_Q0, GLA_K0, GLA_V0, GLA_G0 = 0, 4, 8, 16
HG_Q0, HG_F0, HG_I0, HG_G0 = 24, 32, 40, 48
LR_BLK = 56
LR0 = GLA_G0 * LANE + GLA_WIDTH
PROJ_CHUNK = 512
WPREP_COLS = 512
LEAD_CHUNKS = 2
OUT_ROWS = 256


def _plane_matrix():
    r = np.arange(BLK)[:, None]
    t = np.arange(BLK)[None, :]
    planes = []
    for s in PLANE_LEVELS:
        start = (r // (2 * s)) * (2 * s)
        mid = start + s - 1
        upper = (r - start) >= s
        planes.append(np.where(upper, (t > mid) & (t <= r), (t > r) & (t <= mid)))
    start = (r // SUBLANE) * SUBLANE
    planes.append((t >= start) & (t <= r))
    planes.append((t > r) & (t < start + SUBLANE))
    return np.concatenate(planes, axis=0).astype(np.float32)


def _level_matrix():
    i = np.arange(BLK)[:, None]
    j = np.arange(BLK)[None, :]
    x = i ^ j
    msb = np.zeros_like(x)
    for bit in range(7):
        msb = np.where(x >> bit > 0, bit, msb)
    lvl = (len(LEVELS) - 1) - msb
    lvl = np.where(lvl >= len(MXU_LEVELS), NEAR_CODE + (i - j), lvl)
    lvl = np.where(i == j, DIAG_LEVEL, lvl)
    lvl = np.where(i < j, -1, lvl)
    return lvl.astype(np.int32)


def _wprep_kernel(wt_ref, out_ref):
    j = pl.program_id(0)
    xt = wt_ref[...].astype(out_ref.dtype).T
    lane = lax.broadcasted_iota(jnp.int32, xt.shape, 1)
    keep = jnp.logical_or(j < LR_BLK * LANE // WPREP_COLS, lane < GLA_LOWRANK)
    out_ref[...] = jnp.where(keep, xt, jnp.zeros_like(xt))


def _prepare_in_weight(w_in):
    d, n_in = w_in.shape[1], w_in.shape[2]
    assert n_in == LR_BLK * LANE + GLA_LOWRANK and LR0 % WPREP_COLS == 0
    n_main = LR_BLK * LANE // WPREP_COLS

    def src_row(j):
        shifted = j * WPREP_COLS + GLA_LOWRANK
        row = jnp.where(j < LR0 // WPREP_COLS, j * WPREP_COLS, jnp.where(j < n_main, shifted, LR0))
        return pl.multiple_of(row, GLA_LOWRANK)

    return pl.pallas_call(
        _wprep_kernel,
        grid=(n_main + 1,),
        in_specs=[pl.BlockSpec((pl.Element(WPREP_COLS), pl.Element(d)), lambda j: (src_row(j), 0))],
        out_specs=pl.BlockSpec((d, WPREP_COLS), lambda j: (0, j)),
        out_shape=jax.ShapeDtypeStruct((d, (LR_BLK + 1) * LANE), jnp.bfloat16),
        compiler_params=pltpu.CompilerParams(
            dimension_semantics=("arbitrary",),
            vmem_limit_bytes=VMEM_LIMIT),
        name="in_weight_prep",
    )(w_in[0].T)


def _upper_runs(x, s):
    return jnp.concatenate([x[r:r + s] for r in range(s, BLK, 2 * s)], axis=0)


def _decay_factors(g, planes):
    factors = {s: jnp.exp2(planes[i * BLK:(i + 1) * BLK]) for i, s in enumerate(PLANE_LEVELS)}
    n = len(PLANE_LEVELS)
    groups = range(0, BLK, SUBLANE)
    pre = jnp.exp2(planes[n * BLK:(n + 1) * BLK])
    suf = jnp.exp2(planes[(n + 1) * BLK:(n + 2) * BLK])
    pre = [pre[r:r + SUBLANE] for r in groups]
    suf = [suf[r:r + SUBLANE] for r in groups]
    tot = [p * s for p, s in zip(pre, suf)]
    s = SUBLANE
    while s < BLK:
        factors[s] = jnp.concatenate([pre[v] if (r & s) else suf[v] for v, r in enumerate(groups)], axis=0)
        m = s // SUBLANE
        for v, r in enumerate(groups):
            run = v // m
            if r & s:
                pre[v] = pre[v] * tot[run - 1]
            else:
                suf[v] = suf[v] * tot[run + 1]
        tot = [tot[i] * tot[i + 1] for i in range(0, len(tot), 2)]
        s *= 2
    return factors, jnp.concatenate(pre, axis=0), jnp.concatenate(suf, axis=0), tot[0][0:1, :]


def _prep_block(q, k, g, planes):
    factors, pre, suf, total = _decay_factors(g, planes)
    row = lax.broadcasted_iota(jnp.int32, q.shape, 0)
    back = lambda x, d: jnp.concatenate(
        [pltpu.roll(x[r:r + SUBLANE], d, axis=0) for r in range(0, BLK, SUBLANE)], axis=0)
    near, decay = [], g
    for d in range(1, NEAR_ROWS):
        near.append(jnp.sum(q * back(k, d) * decay, axis=-1, keepdims=True))
        if d < NEAR_ROWS - 1:
            decay = decay * back(g, d)
    lhs, w = [], []
    for s in MXU_LEVELS:
        if s >= SUBLANE:
            qk = jnp.concatenate([(q if (r & s) else k)[r:r + SUBLANE] for r in range(0, BLK, SUBLANE)], axis=0)
            p32 = qk * factors[s]
            lhs.append(_upper_runs(p32, s).astype(jnp.bfloat16))
        else:
            p32 = jnp.where((row & s) != 0, q, k) * factors[s]
            lhs.append(None)
        w.append(p32.astype(jnp.bfloat16))
    return dict(
        lhs=lhs, w=w,
        q_in=(q * pre).astype(jnp.bfloat16),
        k_dec=(k * suf).astype(jnp.bfloat16),
        decay=total,
        dcol=jnp.sum(q * k, axis=-1, keepdims=True),
        near=near)


def _xlu_transpose(x):
    return lax.bitcast_convert_type(lax.bitcast_convert_type(x, jnp.uint16).T, jnp.bfloat16)


def _level_scores(ops):
    out = [[], []]
    for li in range(len(MXU_LEVELS)):
        wt = _xlu_transpose(jnp.concatenate([ops[0]['w'][li], ops[1]['w'][li]], axis=1))
        for hh in range(2):
            lhs = ops[hh]['lhs'][li]
            lhs = ops[hh]['w'][li] if lhs is None else lhs
            out[hh].append(jnp.dot(lhs, wt[hh * LANE:(hh + 1) * LANE], preferred_element_type=jnp.float32))
    return out


def _assemble_scores(g, dcol, near, lvl_ref):
    groups = []
    for r in range(0, BLK, SUBLANE):
        lvl = lvl_ref[r:r + SUBLANE, :]
        a = jnp.where(lvl == DIAG_LEVEL, dcol[r:r + SUBLANE], 0.0)
        for d, col in enumerate(near, start=1):
            a = jnp.where(lvl == NEAR_CODE + d, col[r:r + SUBLANE], a)
        for li, s in enumerate(MXU_LEVELS):
            if s >= SUBLANE:
                if r & s:
                    k = (r // (2 * s)) * s + (r % s)
                    a = jnp.where(lvl == li, g[li][k:k + SUBLANE], a)
            else:
                a = jnp.where(lvl == li, g[li][r:r + SUBLANE], a)
        groups.append(a)
    return jnp.concatenate(groups, axis=0).astype(jnp.bfloat16)


def _finish_block(a, ops, v_bf, st_ref, hh, fresh):
    st = st_ref[hh]
    if fresh is not None:
        st = jnp.where(fresh, 0.0, st)
    o = jnp.dot(a, v_bf, preferred_element_type=jnp.float32)
    o = o + jnp.dot(ops['q_in'], _xlu_transpose(st.astype(jnp.bfloat16)), preferred_element_type=jnp.float32)
    upd = jnp.dot(v_bf.T, ops['k_dec'], preferred_element_type=jnp.float32)
    st_ref[hh] = st * ops['decay'] + upd
    return o


def _head_out(o, gate, gain, dk):
    ms = jnp.mean(o * o, axis=-1, keepdims=True)
    y = o * lax.rsqrt(ms + EPS * dk) * gain
    return y * (gate / (1.0 + jnp.exp(-gate)))


def _sweep(items, fillers, mall_ref, lvl_ref):
    n = len(items)

    def exponents(i):
        logg, g, ks = items[i]['gates']()
        return jnp.dot(mall_ref[...], logg.astype(jnp.bfloat16), preferred_element_type=jnp.float32), g, ks

    def operands(i, planes_g_ks):
        planes, g, ks = planes_g_ks
        return [_prep_block(items[i]['q'](hh), ks[hh], g[:, hh * LANE:(hh + 1) * LANE],
                            planes[:, hh * LANE:(hh + 1) * LANE]) for hh in range(2)]

    planes = {0: exponents(0), 1: exponents(1)}
    ops = {0: operands(0, planes.pop(0))}
    issued = 0
    for i in range(n):
        g = _level_scores(ops[i])
        if i + 2 < n:
            planes[i + 2] = exponents(i + 2)
        while issued < ((i + 1) * len(fillers)) // n:
            fillers[issued]()
            issued += 1
        a = [_assemble_scores(g[hh], ops[i][hh]['dcol'], ops[i][hh]['near'], lvl_ref) for hh in range(2)]
        if i + 1 < n:
            ops[i + 1] = operands(i + 1, planes.pop(i + 1))
        cur = ops.pop(i)
        it = items[i]
        for hh in range(2):
            it['out'](hh, _finish_block(a[hh], cur[hh], it['v'](hh), it['st'], hh, it['fresh']))


def _gla_item(buf, p, c, fresh, wup_ref, bup_ref, gain_ref, st_ref, out_ref, out_row0):
    rows = slice(c * BLK, (c + 1) * BLK)
    pair = slice(p * 2 * LANE, (p + 1) * 2 * LANE)

    def gates():
        z = jnp.dot(buf[LR_BLK, rows, :], wup_ref[:, pair], preferred_element_type=jnp.float32) + bup_ref[:, pair]
        logg = (jnp.minimum(z, 0.0) * LOG2E - jnp.log2(1.0 + jnp.exp(-jnp.abs(z)))) * (1.0 / GLA_TAU)
        return logg, jnp.exp2(logg), [buf[GLA_K0 + 2 * p + hh, rows, :].astype(jnp.float32) for hh in range(2)]

    def wide(first, hh):
        blk = first + 4 * p + 2 * hh
        return jnp.concatenate([buf[blk, rows, :], buf[blk + 1, rows, :]], axis=-1)

    def out(hh, o):
        cols = slice((2 * p + hh) * GLA_HEAD_V, (2 * p + hh + 1) * GLA_HEAD_V)
        y = _head_out(o, wide(GLA_G0, hh).astype(jnp.float32), gain_ref[:, cols], GLA_HEAD_K)
        out_ref[out_row0 + c * BLK:out_row0 + (c + 1) * BLK, cols] = y.astype(out_ref.dtype)

    return dict(gates=gates, q=lambda hh: buf[GLA_Q0 + 2 * p + hh, rows, :].astype(jnp.float32),
                v=lambda hh: wide(GLA_V0, hh), out=out, st=st_ref.at[p], fresh=fresh if c == 0 else None)


def _hgrn_item(buf, p, c, fresh, lb, gain_ref, st_ref, out_ref, out_row0):
    rows = slice(c * BLK, (c + 1) * BLK)
    lb_p = lb[:, p * 2 * LANE:(p + 1) * 2 * LANE]

    def gates():
        hf = jnp.concatenate([buf[HG_F0 + 2 * p, rows, :], buf[HG_F0 + 2 * p + 1, rows, :]],
                             axis=-1).astype(jnp.float32)
        f = lb_p + (1.0 - lb_p) * (1.0 / (1.0 + jnp.exp(-hf)))
        kk = 1.0 - f
        return jnp.log2(f), f, [kk[:, hh * LANE:(hh + 1) * LANE] for hh in range(2)]

    def out(hh, o):
        cols = slice((2 * p + hh) * HGRN_HEAD_V, (2 * p + hh + 1) * HGRN_HEAD_V)
        y = _head_out(o, buf[HG_G0 + 2 * p + hh, rows, :].astype(jnp.float32), gain_ref[:, cols], HGRN_HEAD_K)
        out_ref[out_row0 + c * BLK:out_row0 + (c + 1) * BLK,
                GLA_WIDTH + cols.start:GLA_WIDTH + cols.stop] = y.astype(out_ref.dtype)

    return dict(gates=gates, q=lambda hh: buf[HG_Q0 + 2 * p + hh, rows, :].astype(jnp.float32),
                v=lambda hh: buf[HG_I0 + 2 * p + hh, rows, :], out=out, st=st_ref.at[p],
                fresh=fresh if c == 0 else None)


def _fused_kernel(tiles_per_seq, x0_ref, xn_ref, nw_ref, w_ref, wup_ref, bup_ref, ggain_ref, lb_ref, hgain_ref,
                  mall_ref, lvl_ref, out_ref, buf_new, buf_cur, h_ref, st_gla, st_hg):
    s = pl.program_id(0)

    def normed(x_ref):
        x = x_ref[...]
        ms = jnp.mean(x * x, axis=-1, keepdims=True)
        return (x * lax.rsqrt(ms + EPS) * nw_ref[...]).astype(h_ref.dtype)

    @pl.when(s == 0)
    def _():
        buf_new[...] = jnp.zeros_like(buf_new)
        st_gla[...] = jnp.zeros_like(st_gla)
        st_hg[...] = jnp.zeros_like(st_hg)
        h_ref[...] = normed(x0_ref)

    buf_cur[...] = buf_new[...]

    def project(j0):
        j1 = min(j0 + PROJ_CHUNK, w_ref.shape[1])
        acc = jnp.dot(h_ref[...], w_ref[:, j0:j1], preferred_element_type=jnp.float32)
        for c in range((j1 - j0) // LANE):
            buf_new[j0 // LANE + c] = acc[:, c * LANE:(c + 1) * LANE].astype(buf_new.dtype)

    fillers = [functools.partial(project, j0) for j0 in range(0, w_ref.shape[1], PROJ_CHUNK)]
    for f in fillers[:LEAD_CHUNKS]:
        f()

    lg = lb_ref[...]
    e = jnp.exp(lg - jnp.max(lg, axis=0, keepdims=True))
    lb = e[0:1, :] / jnp.sum(e, axis=0, keepdims=True)

    fresh = lax.rem(s + tiles_per_seq - 1, tiles_per_seq) == 0
    items = []
    for c in range(TILE // BLK):
        for p in range(GLA_HEADS // 2):
            items.append(_gla_item(buf_cur, p, c, fresh, wup_ref, bup_ref, ggain_ref, st_gla, out_ref, 0))
        for p in range(HGRN_HEADS // 2):
            items.append(_hgrn_item(buf_cur, p, c, fresh, lb, hgain_ref, st_hg, out_ref, 0))
    _sweep(items, fillers[LEAD_CHUNKS:], mall_ref, lvl_ref)
    h_ref[...] = normed(xn_ref)


def _fused_projection_recurrence(x2, norm_w, w_all, wup, bup, ggain, lb_logits, hgain, seq):
    bt = x2.shape[0]
    assert bt % TILE == 0 and seq % TILE == 0
    steps = bt // TILE + 1
    mall = jnp.asarray(_plane_matrix(), dtype=jnp.bfloat16)
    lvl = jnp.asarray(_level_matrix())
    const = lambda a: pl.BlockSpec(a.shape, lambda s: (0,) * a.ndim)
    return pl.pallas_call(
        functools.partial(_fused_kernel, seq // TILE),
        grid=(steps,),
        in_specs=[
            pl.BlockSpec((TILE, D_MODEL), lambda s: (0, 0)),
            pl.BlockSpec((TILE, D_MODEL), lambda s: (jnp.minimum(s + 1, steps - 2), 0)),
            const(norm_w),
            pl.BlockSpec(w_all.shape, lambda s: (0, 0), pipeline_mode=pl.Buffered(1)),
            const(wup), const(bup), const(ggain), const(lb_logits), const(hgain), const(mall), const(lvl),
        ],
        out_specs=pl.BlockSpec((TILE, GLA_WIDTH + HGRN_WIDTH), lambda s: (s, 0)),
        out_shape=jax.ShapeDtypeStruct((steps * TILE, GLA_WIDTH + HGRN_WIDTH), jnp.bfloat16),
        scratch_shapes=[
            pltpu.VMEM((LR_BLK + 1, TILE, LANE), jnp.bfloat16),
            pltpu.VMEM((LR_BLK + 1, TILE, LANE), jnp.bfloat16),
            pltpu.VMEM((TILE, D_MODEL), jnp.bfloat16),
            pltpu.VMEM((GLA_HEADS // 2, 2, GLA_HEAD_V, LANE), jnp.float32),
            pltpu.VMEM((HGRN_HEADS // 2, 2, HGRN_HEAD_V, LANE), jnp.float32),
        ],
        compiler_params=pltpu.CompilerParams(
            dimension_semantics=("arbitrary",),
            vmem_limit_bytes=VMEM_LIMIT),
        name="in_proj_recurrence",
    )(x2, x2, norm_w, w_all, wup, bup, ggain, lb_logits, hgain, mall, lvl)


def _out_kernel(m_ref, w_ref, x_ref, fw_ref, out_ref, wb_ref):
    @pl.when(pl.program_id(0) == 0)
    def _():
        wb_ref[...] = w_ref[0].astype(wb_ref.dtype)

    for r0 in range(0, out_ref.shape[0], OUT_ROWS):
        rows = slice(r0, r0 + OUT_ROWS)
        y = jnp.dot(m_ref[rows, :], wb_ref[...], preferred_element_type=jnp.float32) + x_ref[rows, :]
        ms = jnp.mean(y * y, axis=-1, keepdims=True)
        out_ref[rows, :] = y * lax.rsqrt(ms + EPS) * fw_ref[...]


def _output_projection(mixed, row0, wo, x2, final_w, tm=512):
    bt = x2.shape[0]
    d_inner = mixed.shape[1]
    return pl.pallas_call(
        _out_kernel,
        grid=(bt // tm,),
        in_specs=[
            pl.BlockSpec((pl.Element(tm), pl.Element(d_inner)), lambda i: (pl.multiple_of(row0 + i * tm, TILE), 0)),
            pl.BlockSpec((1, d_inner, D_MODEL), lambda i: (0, 0, 0), pipeline_mode=pl.Buffered(1)),
            pl.BlockSpec((tm, D_MODEL), lambda i: (i, 0)),
            pl.BlockSpec((1, D_MODEL), lambda i: (0, 0)),
        ],
        out_specs=pl.BlockSpec((tm, D_MODEL), lambda i: (i, 0)),
        out_shape=jax.ShapeDtypeStruct((bt, D_MODEL), jnp.float32),
        scratch_shapes=[pltpu.VMEM((d_inner, D_MODEL), jnp.bfloat16)],
        compiler_params=pltpu.CompilerParams(
            dimension_semantics=("arbitrary",),
            vmem_limit_bytes=VMEM_LIMIT),
        name="out_proj_residual_norm",
    )(mixed, wo, x2, final_w)


def kernel(x, norm_w, w_in, gla_w_up, gla_b_up, gla_norm_w, hgrn_lb_logits, hgrn_norm_w, w_out, final_norm_w):
    batch, seq, d = x.shape
    assert d == D_MODEL and norm_w.shape[0] == 1
    x2 = x.reshape(batch * seq, d)
    wup = jnp.pad(gla_w_up[0], ((0, LANE - GLA_LOWRANK), (0, 0))).astype(jnp.bfloat16)
    mixed = _fused_projection_recurrence(x2, norm_w, _prepare_in_weight(w_in), wup, gla_b_up, gla_norm_w,
                                         hgrn_lb_logits, hgrn_norm_w, seq)
    out = _output_projection(mixed, TILE, w_out, x2, final_norm_w.reshape(1, d))
    return out.reshape(batch, seq, d)
```

```python
import functools

import numpy as np
import jax
import jax.numpy as jnp
from jax import lax
from jax.experimental import pallas as pl
from jax.experimental.pallas import tpu as pltpu

D_MODEL = 2048
GLA_HEADS = 4
GLA_HEAD_K = 128
GLA_HEAD_V = 256
GLA_KEY_WIDTH = GLA_HEADS * GLA_HEAD_K
GLA_WIDTH = GLA_HEADS * GLA_HEAD_V
GLA_LOWRANK = 16
GLA_TAU = 16.0
HGRN_HEADS = 8
HGRN_HEAD_K = 128
HGRN_HEAD_V = 128
HGRN_WIDTH = HGRN_HEADS * HGRN_HEAD_V
EPS = 1e-6

LANE = 128
SUBLANE = 8
BLK = 128
TILE = 2 * BLK
LEVELS = (64, 32, 16, 8, 4, 2, 1)
MXU_LEVELS = tuple(s for s in LEVELS if s >= SUBLANE)
N_PLANES = 2
DIAG_LEVEL = len(LEVELS)
NEAR_CODE = 16
VMEM_LIMIT = 56 * 1024 * 1024
LOG2E = 1.4426950408889634

GLA_Q0, GLA_K0, GLA_V0, GLA_G0 = 0, 4, 8, 16
HG_Q0, HG_F0, HG_I0, HG_G0 = 24, 32, 40, 48
LR_BLK = 56
LR0 = GLA_G0 * LANE + GLA_WIDTH
PROJ_CHUNK = 512
WPREP_COLS = 512
LEAD_CHUNKS = 2
TAIL_CHUNKS = 1
OUT_ROWS = 256


def _plane_matrix():
    r = np.arange(BLK)[:, None]
    t = np.arange(BLK)[None, :]
    start = (r // SUBLANE) * SUBLANE
    prefix = (t >= start) & (t <= r)
    suffix = (t > r) & (t < start + SUBLANE)
    return np.concatenate([prefix, suffix], axis=0).astype(np.float32)


def _level_matrix():
    i = np.arange(BLK)[:, None]
    j = np.arange(BLK)[None, :]
    x = i ^ j
    msb = np.zeros_like(x)
    for bit in range(7):
        msb = np.where(x >> bit > 0, bit, msb)
    lvl = (len(LEVELS) - 1) - msb
    lvl = np.where(lvl >= len(MXU_LEVELS), NEAR_CODE + (i - j), lvl)
    lvl = np.where(i == j, DIAG_LEVEL, lvl)
    lvl = np.where(i < j, -1, lvl)
    return lvl.astype(np.int32)


def _wprep_kernel(wt_ref, out_ref):
    j = pl.program_id(0)
    xt = wt_ref[...].astype(out_ref.dtype).T
    lane = lax.broadcasted_iota(jnp.int32, xt.shape, 1)
    keep = jnp.logical_or(j < LR_BLK * LANE // WPREP_COLS, lane < GLA_LOWRANK)
    out_ref[...] = jnp.where(keep, xt, jnp.zeros_like(xt))


def _prepare_in_weight(w_in):
    d, n_in = w_in.shape[1], w_in.shape[2]
    assert n_in == LR_BLK * LANE + GLA_LOWRANK and LR0 % WPREP_COLS == 0
    n_main = LR_BLK * LANE // WPREP_COLS

    def src_row(j):
        shifted = j * WPREP_COLS + GLA_LOWRANK
        row = jnp.where(j < LR0 // WPREP_COLS, j * WPREP_COLS, jnp.where(j < n_main, shifted, LR0))
        return pl.multiple_of(row, GLA_LOWRANK)

    return pl.pallas_call(
        _wprep_kernel,
        grid=(n_main + 1,),
        in_specs=[pl.BlockSpec((pl.Element(WPREP_COLS), pl.Element(d)), lambda j: (src_row(j), 0))],
        out_specs=pl.BlockSpec((d, WPREP_COLS), lambda j: (0, j)),
        out_shape=jax.ShapeDtypeStruct((d, (LR_BLK + 1) * LANE), jnp.bfloat16),
        compiler_params=pltpu.CompilerParams(
            dimension_semantics=("arbitrary",),
            vmem_limit_bytes=VMEM_LIMIT),
        name="in_weight_prep",
    )(w_in[0].T)


def _upper_runs(x, s):
    return jnp.concatenate([x[r:r + s] for r in range(s, BLK, 2 * s)], axis=0)


def _decay_factors(planes):
    groups = range(0, BLK, SUBLANE)
    pre = jnp.exp2(planes[:BLK])
    suf = jnp.exp2(planes[BLK:])
    pre = [pre[r:r + SUBLANE] for r in groups]
    suf = [suf[r:r + SUBLANE] for r in groups]
    tot = [p * s for p, s in zip(pre, suf)]
    factors = {}
    s = SUBLANE
    while s < BLK:
        factors[s] = jnp.concatenate([pre[v] if (r & s) else suf[v] for v, r in enumerate(groups)], axis=0)
        m = s // SUBLANE
        for v, r in enumerate(groups):
            run = v // m
            if r & s:
                pre[v] = pre[v] * tot[run - 1]
            else:
                suf[v] = suf[v] * tot[run + 1]
        tot = [tot[i] * tot[i + 1] for i in range(0, len(tot), 2)]
        s *= 2
    return factors, jnp.concatenate(pre, axis=0), jnp.concatenate(suf, axis=0), tot[0][0:1, :]


def _prep_block(q, k, g, planes):
    factors, pre, suf, total = _decay_factors(planes)
    back = lambda x, d: jnp.concatenate(
        [pltpu.roll(x[r:r + SUBLANE], d, axis=0) for r in range(0, BLK, SUBLANE)], axis=0)
    near, decay = [], g
    for d in range(1, SUBLANE):
        near.append(jnp.sum(q * back(k, d) * decay, axis=-1, keepdims=True))
        if d < SUBLANE - 1:
            decay = decay * back(g, d)
    lhs, w = [], []
    for s in MXU_LEVELS:
        qk = jnp.concatenate([(q if (r & s) else k)[r:r + SUBLANE] for r in range(0, BLK, SUBLANE)], axis=0)
        p32 = qk * factors[s]
        lhs.append(_upper_runs(p32, s).astype(jnp.bfloat16))
        w.append(p32.astype(jnp.bfloat16))
    return dict(
        lhs=lhs, w=w,
        q_in=(q * pre).astype(jnp.bfloat16),
        k_dec=(k * suf).astype(jnp.bfloat16),
        decay=total,
        dcol=jnp.sum(q * k, axis=-1, keepdims=True),
        near=near)


def _xlu_transpose(x):
    return lax.bitcast_convert_type(lax.bitcast_convert_type(x, jnp.uint16).T, jnp.bfloat16)


def _level_scores(ops):
    out = [[], []]
    for li in range(len(MXU_LEVELS)):
        wt = _xlu_transpose(jnp.concatenate([ops[0]['w'][li], ops[1]['w'][li]], axis=1))
        for hh in range(2):
            out[hh].append(jnp.dot(ops[hh]['lhs'][li], wt[hh * LANE:(hh + 1) * LANE],
                                   preferred_element_type=jnp.float32))
    return out


def _assemble_scores(g, dcol, near, lvl_ref):
    groups = []
    for r in range(0, BLK, SUBLANE):
        lvl = lvl_ref[r:r + SUBLANE, :]
        a = jnp.where(lvl == DIAG_LEVEL, dcol[r:r + SUBLANE], 0.0)
        for d, col in enumerate(near, start=1):
            a = jnp.where(lvl == NEAR_CODE + d, col[r:r + SUBLANE], a)
        for li, s in enumerate(MXU_LEVELS):
            if r & s:
                k = (r // (2 * s)) * s + (r % s)
                a = jnp.where(lvl == li, g[li][k:k + SUBLANE], a)
        groups.append(a)
    return jnp.concatenate(groups, axis=0).astype(jnp.bfloat16)


def _finish_block(a, ops, v_bf, st_ref, hh, fresh):
    st = st_ref[hh]
    if fresh is not None:
        st = jnp.where(fresh, 0.0, st)
    o = jnp.dot(jnp.concatenate([a, ops['q_in']], axis=1),
                jnp.concatenate([v_bf, _xlu_transpose(st.astype(jnp.bfloat16))], axis=0),
                preferred_element_type=jnp.float32)
    upd = jnp.dot(v_bf.T, ops['k_dec'], preferred_element_type=jnp.float32)
    st_ref[hh] = st * ops['decay'] + upd
    return o


def _head_out(o, gate, gain, dk):
    ms = jnp.mean(o * o, axis=-1, keepdims=True)
    y = o * lax.rsqrt(ms + EPS * dk) * gain
    return y * (gate / (1.0 + jnp.exp(-gate)))


def _sweep(items, fillers, mall_ref, lvl_ref):
    n = len(items)

    def exponents(i):
        logg, g, ks = items[i]['gates']()
        return jnp.dot(mall_ref[...], logg.astype(jnp.bfloat16), preferred_element_type=jnp.float32), g, ks

    def operands(i, planes_g_ks):
        planes, g, ks = planes_g_ks
        return [_prep_block(items[i]['q'](hh), ks[hh], g[:, hh * LANE:(hh + 1) * LANE],
                            planes[:, hh * LANE:(hh + 1) * LANE]) for hh in range(2)]

    planes = {0: exponents(0), 1: exponents(1)}
    ops = {0: operands(0, planes.pop(0))}
    issued = 0
    for i in range(n):
        g = _level_scores(ops[i])
        if i + 2 < n:
            planes[i + 2] = exponents(i + 2)
        while issued < ((i + 1) * (len(fillers) - TAIL_CHUNKS)) // n:
            fillers[issued]()
            issued += 1
        a = [_assemble_scores(g[hh], ops[i][hh]['dcol'], ops[i][hh]['near'], lvl_ref) for hh in range(2)]
        if i + 1 < n:
            ops[i + 1] = operands(i + 1, planes.pop(i + 1))
        cur = ops.pop(i)
        it = items[i]
        for hh in range(2):
            it['out'](hh, _finish_block(a[hh], cur[hh], it['v'](hh), it['st'], hh, it['fresh']))
    for f in fillers[issued:]:
        f()


def _gla_item(buf, p, c, fresh, wup_ref, bup_ref, gain_ref, st_ref, out_ref, out_row0):
    rows = slice(c * BLK, (c + 1) * BLK)
    pair = slice(p * 2 * LANE, (p + 1) * 2 * LANE)

    def gates():
        z = jnp.dot(buf[LR_BLK, rows, :], wup_ref[:, pair], preferred_element_type=jnp.float32) + bup_ref[:, pair]
        logg = (jnp.minimum(z, 0.0) * LOG2E - jnp.log2(1.0 + jnp.exp(-jnp.abs(z)))) * (1.0 / GLA_TAU)
        return logg, jnp.exp2(logg), [buf[GLA_K0 + 2 * p + hh, rows, :].astype(jnp.float32) for hh in range(2)]

    def wide(first, hh):
        blk = first + 4 * p + 2 * hh
        return jnp.concatenate([buf[blk, rows, :], buf[blk + 1, rows, :]], axis=-1)

    def out(hh, o):
        cols = slice((2 * p + hh) * GLA_HEAD_V, (2 * p + hh + 1) * GLA_HEAD_V)
        y = _head_out(o, wide(GLA_G0, hh).astype(jnp.float32), gain_ref[:, cols], GLA_HEAD_K)
        out_ref[out_row0 + c * BLK:out_row0 + (c + 1) * BLK, cols] = y.astype(out_ref.dtype)

    return dict(gates=gates, q=lambda hh: buf[GLA_Q0 + 2 * p + hh, rows, :].astype(jnp.float32),
                v=lambda hh: wide(GLA_V0, hh), out=out, st=st_ref.at[p], fresh=fresh if c == 0 else None)


def _hgrn_item(buf, p, c, fresh, lb, gain_ref, st_ref, out_ref, out_row0):
    rows = slice(c * BLK, (c + 1) * BLK)
    lb_p = lb[:, p * 2 * LANE:(p + 1) * 2 * LANE]

    def gates():
        hf = jnp.concatenate([buf[HG_F0 + 2 * p, rows, :], buf[HG_F0 + 2 * p + 1, rows, :]],
                             axis=-1).astype(jnp.float32)
        f = lb_p + (1.0 - lb_p) * (1.0 / (1.0 + jnp.exp(-hf)))
        kk = 1.0 - f
        return jnp.log2(f), f, [kk[:, hh * LANE:(hh + 1) * LANE] for hh in range(2)]

    def out(hh, o):
        cols = slice((2 * p + hh) * HGRN_HEAD_V, (2 * p + hh + 1) * HGRN_HEAD_V)
        y = _head_out(o, buf[HG_G0 + 2 * p + hh, rows, :].astype(jnp.float32), gain_ref[:, cols], HGRN_HEAD_K)
        out_ref[out_row0 + c * BLK:out_row0 + (c + 1) * BLK,
                GLA_WIDTH + cols.start:GLA_WIDTH + cols.stop] = y.astype(out_ref.dtype)

    return dict(gates=gates, q=lambda hh: buf[HG_Q0 + 2 * p + hh, rows, :].astype(jnp.float32),
                v=lambda hh: buf[HG_I0 + 2 * p + hh, rows, :], out=out, st=st_ref.at[p],
                fresh=fresh if c == 0 else None)


def _fused_kernel(tiles_per_seq, x0_ref, xn_ref, nw_ref, w_ref, wup_ref, bup_ref, ggain_ref, lb_ref, hgain_ref,
                  mall_ref, lvl_ref, out_ref, buf_new, buf_cur, h_ref, st_gla, st_hg):
    s = pl.program_id(0)

    def normed(x_ref):
        x = x_ref[...]
        ms = jnp.mean(x * x, axis=-1, keepdims=True)
        return (x * lax.rsqrt(ms + EPS) * nw_ref[...]).astype(h_ref.dtype)

    @pl.when(s == 0)
    def _():
        buf_new[...] = jnp.zeros_like(buf_new)
        st_gla[...] = jnp.zeros_like(st_gla)
        st_hg[...] = jnp.zeros_like(st_hg)
        h_ref[...] = normed(x0_ref)

    buf_cur[...] = buf_new[...]

    def project(j0):
        j1 = min(j0 + PROJ_CHUNK, w_ref.shape[1])
        acc = jnp.dot(h_ref[...], w_ref[:, j0:j1], preferred_element_type=jnp.float32)
        for c in range((j1 - j0) // LANE):
            buf_new[j0 // LANE + c] = acc[:, c * LANE:(c + 1) * LANE].astype(buf_new.dtype)

    fillers = [functools.partial(project, j0) for j0 in range(0, w_ref.shape[1], PROJ_CHUNK)]
    for f in fillers[:LEAD_CHUNKS]:
        f()

    lg = lb_ref[...]
    e = jnp.exp(lg - jnp.max(lg, axis=0, keepdims=True))
    lb = e[0:1, :] / jnp.sum(e, axis=0, keepdims=True)

    fresh = lax.rem(s + tiles_per_seq - 1, tiles_per_seq) == 0
    items = []
    for c in range(TILE // BLK):
        for p in range(GLA_HEADS // 2):
            items.append(_gla_item(buf_cur, p, c, fresh, wup_ref, bup_ref, ggain_ref, st_gla, out_ref, 0))
        for p in range(HGRN_HEADS // 2):
            items.append(_hgrn_item(buf_cur, p, c, fresh, lb, hgain_ref, st_hg, out_ref, 0))
    _sweep(items, fillers[LEAD_CHUNKS:], mall_ref, lvl_ref)
    h_ref[...] = normed(xn_ref)


def _fused_projection_recurrence(x2, norm_w, w_all, wup, bup, ggain, lb_logits, hgain, seq):
    bt = x2.shape[0]
    assert bt % TILE == 0 and seq % TILE == 0
    steps = bt // TILE + 1
    mall = jnp.asarray(_plane_matrix(), dtype=jnp.bfloat16)
    lvl = jnp.asarray(_level_matrix())
    const = lambda a: pl.BlockSpec(a.shape, lambda s: (0,) * a.ndim)
    return pl.pallas_call(
        functools.partial(_fused_kernel, seq // TILE),
        grid=(steps,),
        in_specs=[
            pl.BlockSpec((TILE, D_MODEL), lambda s: (0, 0)),
            pl.BlockSpec((TILE, D_MODEL), lambda s: (jnp.minimum(s + 1, steps - 2), 0)),
            const(norm_w),
            pl.BlockSpec(w_all.shape, lambda s: (0, 0), pipeline_mode=pl.Buffered(1)),
            const(wup), const(bup), const(ggain), const(lb_logits), const(hgain), const(mall), const(lvl),
        ],
        out_specs=pl.BlockSpec((TILE, GLA_WIDTH + HGRN_WIDTH), lambda s: (s, 0)),
        out_shape=jax.ShapeDtypeStruct((steps * TILE, GLA_WIDTH + HGRN_WIDTH), jnp.bfloat16),
        scratch_shapes=[
            pltpu.VMEM((LR_BLK + 1, TILE, LANE), jnp.bfloat16),
            pltpu.VMEM((LR_BLK + 1, TILE, LANE), jnp.bfloat16),
            pltpu.VMEM((TILE, D_MODEL), jnp.bfloat16),
            pltpu.VMEM((GLA_HEADS // 2, 2, GLA_HEAD_V, LANE), jnp.float32),
            pltpu.VMEM((HGRN_HEADS // 2, 2, HGRN_HEAD_V, LANE), jnp.float32),
        ],
        compiler_params=pltpu.CompilerParams(
            dimension_semantics=("arbitrary",),
            vmem_limit_bytes=VMEM_LIMIT),
        name="in_proj_recurrence",
    )(x2, x2, norm_w, w_all, wup, bup, ggain, lb_logits, hgain, mall, lvl)


def _out_kernel(m_ref, w_ref, x_ref, fw_ref, out_ref, wb_ref):
    @pl.when(pl.program_id(0) == 0)
    def _():
        wb_ref[...] = w_ref[0].astype(wb_ref.dtype)

    for r0 in range(0, out_ref.shape[0], OUT_ROWS):
        rows = slice(r0, r0 + OUT_ROWS)
        y = jnp.dot(m_ref[rows, :], wb_ref[...], preferred_element_type=jnp.float32) + x_ref[rows, :]
        ms = jnp.mean(y * y, axis=-1, keepdims=True)
        out_ref[rows, :] = y * lax.rsqrt(ms + EPS) * fw_ref[...]


def _output_projection(mixed, row0, wo, x2, final_w, tm=512):
    bt = x2.shape[0]
    d_inner = mixed.shape[1]
    return pl.pallas_call(
        _out_kernel,
        grid=(bt // tm,),
        in_specs=[
            pl.BlockSpec((pl.Element(tm), pl.Element(d_inner)), lambda i: (pl.multiple_of(row0 + i * tm, TILE), 0)),
            pl.BlockSpec((1, d_inner, D_MODEL), lambda i: (0, 0, 0), pipeline_mode=pl.Buffered(1)),
            pl.BlockSpec((tm, D_MODEL), lambda i: (i, 0)),
            pl.BlockSpec((1, D_MODEL), lambda i: (0, 0)),
        ],
        out_specs=pl.BlockSpec((tm, D_MODEL), lambda i: (i, 0)),
        out_shape=jax.ShapeDtypeStruct((bt, D_MODEL), jnp.float32),
        scratch_shapes=[pltpu.VMEM((d_inner, D_MODEL), jnp.bfloat16)],
        compiler_params=pltpu.CompilerParams(
            dimension_semantics=("arbitrary",),
            vmem_limit_bytes=VMEM_LIMIT),
        name="out_proj_residual_norm",
    )(mixed, wo, x2, final_w)


def kernel(x, norm_w, w_in, gla_w_up, gla_b_up, gla_norm_w, hgrn_lb_logits, hgrn_norm_w, w_out, final_norm_w):
    batch, seq, d = x.shape
    assert d == D_MODEL and norm_w.shape[0] == 1
    x2 = x.reshape(batch * seq, d)
    wup = jnp.pad(gla_w_up[0], ((0, LANE - GLA_LOWRANK), (0, 0))).astype(jnp.bfloat16)
    mixed = _fused_projection_recurrence(x2, norm_w, _prepare_in_weight(w_in), wup, gla_b_up, gla_norm_w,
                                         hgrn_lb_logits, hgrn_norm_w, seq)
    out = _output_projection(mixed, TILE, w_out, x2, final_norm_w.reshape(1, d))
    return out.reshape(batch, seq, d)
```

```python
import functools

import numpy as np
import jax
import jax.numpy as jnp
from jax import lax
from jax.experimental import pallas as pl
from jax.experimental.pallas import tpu as pltpu

D_MODEL = 2048
GLA_HEADS = 4
GLA_HEAD_K = 128
GLA_HEAD_V = 256
GLA_KEY_WIDTH = GLA_HEADS * GLA_HEAD_K
GLA_WIDTH = GLA_HEADS * GLA_HEAD_V
GLA_LOWRANK = 16
GLA_TAU = 16.0
HGRN_HEADS = 8
HGRN_HEAD_K = 128
HGRN_HEAD_V = 128
HGRN_WIDTH = HGRN_HEADS * HGRN_HEAD_V
EPS = 1e-6

LANE = 128
SUBLANE = 8
BLK = 128
TILE = 2 * BLK
LEVELS = (64, 32, 16, 8, 4, 2, 1)
MXU_LEVELS = tuple(s for s in LEVELS if s >= SUBLANE)
N_PLANES = 2
DIAG_LEVEL = len(LEVELS)
NEAR_CODE = 16
VMEM_LIMIT = 56 * 1024 * 1024
LOG2E = 1.4426950408889634

GLA_Q0, GLA_K0, GLA_V0, GLA_G0 = 0, 4, 8, 16
HG_Q0, HG_F0, HG_I0, HG_G0 = 24, 32, 40, 48
LR_BLK = 56
LR0 = GLA_G0 * LANE + GLA_WIDTH
PROJ_CHUNK = 512
WPREP_COLS = 512
LEAD_CHUNKS = 2
TAIL_CHUNKS = 1
OUT_ROWS = 256


def _plane_matrix():
    r = np.arange(BLK)[:, None]
    t = np.arange(BLK)[None, :]
    start = (r // SUBLANE) * SUBLANE
    prefix = (t >= start) & (t <= r)
    suffix = (t > r) & (t < start + SUBLANE)
    return np.concatenate([prefix, suffix], axis=0).astype(np.float32)


def _level_matrix():
    i = np.arange(BLK)[:, None]
    j = np.arange(BLK)[None, :]
    x = i ^ j
    msb = np.zeros_like(x)
    for bit in range(7):
        msb = np.where(x >> bit > 0, bit, msb)
    lvl = (len(LEVELS) - 1) - msb
    lvl = np.where(lvl >= len(MXU_LEVELS), NEAR_CODE + (i - j), lvl)
    lvl = np.where(i == j, DIAG_LEVEL, lvl)
    lvl = np.where(i < j, -1, lvl)
    return lvl.astype(np.int32)


def _wprep_kernel(wt_ref, out_ref):
    j = pl.program_id(0)
    xt = wt_ref[...].astype(out_ref.dtype).T
    lane = lax.broadcasted_iota(jnp.int32, xt.shape, 1)
    keep = jnp.logical_or(j < LR_BLK * LANE // WPREP_COLS, lane < GLA_LOWRANK)
    out_ref[...] = jnp.where(keep, xt, jnp.zeros_like(xt))


def _prepare_in_weight(w_in):
    d, n_in = w_in.shape[1], w_in.shape[2]
    assert n_in == LR_BLK * LANE + GLA_LOWRANK and LR0 % WPREP_COLS == 0
    n_main = LR_BLK * LANE // WPREP_COLS

    def src_row(j):
        shifted = j * WPREP_COLS + GLA_LOWRANK
        row = jnp.where(j < LR0 // WPREP_COLS, j * WPREP_COLS, jnp.where(j < n_main, shifted, LR0))
        return pl.multiple_of(row, GLA_LOWRANK)

    return pl.pallas_call(
        _wprep_kernel,
        grid=(n_main + 1,),
        in_specs=[pl.BlockSpec((pl.Element(WPREP_COLS), pl.Element(d)), lambda j: (src_row(j), 0))],
        out_specs=pl.BlockSpec((d, WPREP_COLS), lambda j: (0, j)),
        out_shape=jax.ShapeDtypeStruct((d, (LR_BLK + 1) * LANE), jnp.bfloat16),
        compiler_params=pltpu.CompilerParams(
            dimension_semantics=("arbitrary",),
            vmem_limit_bytes=VMEM_LIMIT),
        name="in_weight_prep",
    )(w_in[0].T)


def _upper_runs(x, s):
    return jnp.concatenate([x[r:r + s] for r in range(s, BLK, 2 * s)], axis=0)


def _decay_factors(planes):
    groups = range(0, BLK, SUBLANE)
    pre = jnp.exp2(planes[:BLK])
    suf = jnp.exp2(planes[BLK:])
    pre = [pre[r:r + SUBLANE] for r in groups]
    suf = [suf[r:r + SUBLANE] for r in groups]
    tot = [p * s for p, s in zip(pre, suf)]
    factors = {}
    s = SUBLANE
    while s < BLK:
        factors[s] = jnp.concatenate([pre[v] if (r & s) else suf[v] for v, r in enumerate(groups)], axis=0)
        m = s // SUBLANE
        for v, r in enumerate(groups):
            run = v // m
            if r & s:
                pre[v] = pre[v] * tot[run - 1]
            else:
                suf[v] = suf[v] * tot[run + 1]
        tot = [tot[i] * tot[i + 1] for i in range(0, len(tot), 2)]
        s *= 2
    return factors, jnp.concatenate(pre, axis=0), jnp.concatenate(suf, axis=0), tot[0][0:1, :]


def _prep_block(q, k, g, planes):
    factors, pre, suf, total = _decay_factors(planes)
    back = lambda x, d: jnp.concatenate(
        [pltpu.roll(x[r:r + SUBLANE], d, axis=0) for r in range(0, BLK, SUBLANE)], axis=0)
    near, decay = [], g
    for d in range(1, SUBLANE):
        near.append(jnp.sum(q * back(k, d) * decay, axis=-1, keepdims=True))
        if d < SUBLANE - 1:
            decay = decay * back(g, d)
    lhs, w = [], []
    for s in MXU_LEVELS:
        qk = jnp.concatenate([(q if (r & s) else k)[r:r + SUBLANE] for r in range(0, BLK, SUBLANE)], axis=0)
        p32 = qk * factors[s]
        lhs.append(_upper_runs(p32, s).astype(jnp.bfloat16))
        w.append(p32.astype(jnp.bfloat16))
    return dict(
        lhs=lhs, w=w,
        q_in=(q * pre).astype(jnp.bfloat16),
        k_dec=(k * suf).astype(jnp.bfloat16),
        decay=total,
        dcol=jnp.sum(q * k, axis=-1, keepdims=True),
        near=near)


def _xlu_transpose(x):
    return lax.bitcast_convert_type(lax.bitcast_convert_type(x, jnp.uint16).T, jnp.bfloat16)


def _level_scores(ops):
    out = [[], []]
    for li in range(len(MXU_LEVELS)):
        wt = _xlu_transpose(jnp.concatenate([ops[0]['w'][li], ops[1]['w'][li]], axis=1))
        for hh in range(2):
            out[hh].append(jnp.dot(ops[hh]['lhs'][li], wt[hh * LANE:(hh + 1) * LANE],
                                   preferred_element_type=jnp.float32))
    return out


def _assemble_scores(g, dcol, near, lvl_ref):
    groups = []
    for r in range(0, BLK, SUBLANE):
        lvl = lvl_ref[r:r + SUBLANE, :]
        a = jnp.where(lvl == DIAG_LEVEL, dcol[r:r + SUBLANE], 0.0)
        for d, col in enumerate(near, start=1):
            a = jnp.where(lvl == NEAR_CODE + d, col[r:r + SUBLANE], a)
        for li, s in enumerate(MXU_LEVELS):
            if r & s:
                k = (r // (2 * s)) * s + (r % s)
                a = jnp.where(lvl == li, g[li][k:k + SUBLANE], a)
        groups.append(a)
    return jnp.concatenate(groups, axis=0).astype(jnp.bfloat16)


def _finish_block(a, ops, v_bf, st_ref, hh, fresh):
    st = st_ref[hh]
    if fresh is not None:
        st = jnp.where(fresh, 0.0, st)
    o = jnp.dot(jnp.concatenate([a, ops['q_in']], axis=1),
                jnp.concatenate([v_bf, _xlu_transpose(st.astype(jnp.bfloat16))], axis=0),
                preferred_element_type=jnp.float32)
    upd = jnp.dot(v_bf.T, ops['k_dec'], preferred_element_type=jnp.float32)
    st_ref[hh] = st * ops['decay'] + upd
    return o


def _head_out(o, gate, gain, dk):
    ms = jnp.mean(o * o, axis=-1, keepdims=True)
    y = o * lax.rsqrt(ms + EPS * dk) * gain
    return y * (gate / (1.0 + jnp.exp(-gate)))


def _sweep(items, fillers, mall_ref, lvl_ref):
    n = len(items)

    def exponents(i):
        logg, g, ks = items[i]['gates']()
        return jnp.dot(mall_ref[...], logg.astype(jnp.bfloat16), preferred_element_type=jnp.float32), g, ks

    def operands(i, planes_g_ks):
        planes, g, ks = planes_g_ks
        return [_prep_block(items[i]['q'](hh), ks[hh], g[:, hh * LANE:(hh + 1) * LANE],
                            planes[:, hh * LANE:(hh + 1) * LANE]) for hh in range(2)]

    planes = {0: exponents(0), 1: exponents(1)}
    ops = {0: operands(0, planes.pop(0))}
    issued = 0
    for i in range(n):
        g = _level_scores(ops[i])
        if i + 2 < n:
            planes[i + 2] = exponents(i + 2)
        while issued < ((i + 1) * (len(fillers) - TAIL_CHUNKS)) // n:
            fillers[issued]()
            issued += 1
        a = [_assemble_scores(g[hh], ops[i][hh]['dcol'], ops[i][hh]['near'], lvl_ref) for hh in range(2)]
        if i + 1 < n:
            ops[i + 1] = operands(i + 1, planes.pop(i + 1))
        cur = ops.pop(i)
        it = items[i]
        for hh in range(2):
            it['out'](hh, _finish_block(a[hh], cur[hh], it['v'](hh), it['st'], hh, it['fresh']))
    for f in fillers[issued:]:
        f()


def _gla_item(buf, p, c, fresh, wup_ref, bup_ref, gain_ref, st_ref, out_ref, out_row0):
    rows = slice(c * BLK, (c + 1) * BLK)
    pair = slice(p * 2 * LANE, (p + 1) * 2 * LANE)

    def gates():
        z = jnp.dot(buf[LR_BLK, rows, :], wup_ref[:, pair], preferred_element_type=jnp.float32) + bup_ref[:, pair]
        logg = (jnp.minimum(z, 0.0) * LOG2E - jnp.log2(1.0 + jnp.exp(-jnp.abs(z)))) * (1.0 / GLA_TAU)
        return logg, jnp.exp2(logg), [buf[GLA_K0 + 2 * p + hh, rows, :].astype(jnp.float32) for hh in range(2)]

    def wide(first, hh):
        blk = first + 4 * p + 2 * hh
        return jnp.concatenate([buf[blk, rows, :], buf[blk + 1, rows, :]], axis=-1)

    def out(hh, o):
        cols = slice((2 * p + hh) * GLA_HEAD_V, (2 * p + hh + 1) * GLA_HEAD_V)
        y = _head_out(o, wide(GLA_G0, hh).astype(jnp.float32), gain_ref[:, cols], GLA_HEAD_K)
        out_ref[out_row0 + c * BLK:out_row0 + (c + 1) * BLK, cols] = y.astype(out_ref.dtype)

    return dict(gates=gates, q=lambda hh: buf[GLA_Q0 + 2 * p + hh, rows, :].astype(jnp.float32),
                v=lambda hh: wide(GLA_V0, hh), out=out, st=st_ref.at[p], fresh=fresh if c == 0 else None)


def _hgrn_item(buf, p, c, fresh, lb, gain_ref, st_ref, out_ref, out_row0):
    rows = slice(c * BLK, (c + 1) * BLK)
    lb_p = lb[:, p * 2 * LANE:(p + 1) * 2 * LANE]

    def gates():
        hf = jnp.concatenate([buf[HG_F0 + 2 * p, rows, :], buf[HG_F0 + 2 * p + 1, rows, :]],
                             axis=-1).astype(jnp.float32)
        f = lb_p + (1.0 - lb_p) * (1.0 / (1.0 + jnp.exp(-hf)))
        kk = 1.0 - f
        return jnp.log2(f), f, [kk[:, hh * LANE:(hh + 1) * LANE] for hh in range(2)]

    def out(hh, o):
        cols = slice((2 * p + hh) * HGRN_HEAD_V, (2 * p + hh + 1) * HGRN_HEAD_V)
        y = _head_out(o, buf[HG_G0 + 2 * p + hh, rows, :].astype(jnp.float32), gain_ref[:, cols], HGRN_HEAD_K)
        out_ref[out_row0 + c * BLK:out_row0 + (c + 1) * BLK,
                GLA_WIDTH + cols.start:GLA_WIDTH + cols.stop] = y.astype(out_ref.dtype)

    return dict(gates=gates, q=lambda hh: buf[HG_Q0 + 2 * p + hh, rows, :].astype(jnp.float32),
                v=lambda hh: buf[HG_I0 + 2 * p + hh, rows, :], out=out, st=st_ref.at[p],
                fresh=fresh if c == 0 else None)


def _fused_kernel(tiles_per_seq, x0_ref, xn_ref, nw_ref, w_ref, wup_ref, bup_ref, ggain_ref, lb_ref, hgain_ref,
                  mall_ref, lvl_ref, out_ref, buf_new, buf_cur, h_ref, st_gla, st_hg):
    s = pl.program_id(0)

    def normed(x_ref):
        x = x_ref[...]
        ms = jnp.mean(x * x, axis=-1, keepdims=True)
        return (x * lax.rsqrt(ms + EPS) * nw_ref[...]).astype(h_ref.dtype)

    @pl.when(s == 0)
    def _():
        buf_new[...] = jnp.zeros_like(buf_new)
        st_gla[...] = jnp.zeros_like(st_gla)
        st_hg[...] = jnp.zeros_like(st_hg)
        h_ref[...] = normed(x0_ref)

    buf_cur[...] = buf_new[...]

    def project(j0):
        j1 = min(j0 + PROJ_CHUNK, w_ref.shape[1])
        acc = jnp.dot(h_ref[...], w_ref[:, j0:j1], preferred_element_type=jnp.float32)
        for c in range((j1 - j0) // LANE):
            buf_new[j0 // LANE + c] = acc[:, c * LANE:(c + 1) * LANE].astype(buf_new.dtype)

    fillers = [functools.partial(project, j0) for j0 in range(0, w_ref.shape[1], PROJ_CHUNK)]
    for f in fillers[:LEAD_CHUNKS]:
        f()

    lg = lb_ref[...]
    e = jnp.exp(lg - jnp.max(lg, axis=0, keepdims=True))
    lb = e[0:1, :] / jnp.sum(e, axis=0, keepdims=True)

    fresh = lax.rem(s + tiles_per_seq - 1, tiles_per_seq) == 0
    items = []
    for c in range(TILE // BLK):
        for p in range(GLA_HEADS // 2):
            items.append(_gla_item(buf_cur, p, c, fresh, wup_ref, bup_ref, ggain_ref, st_gla, out_ref, 0))
        for p in range(HGRN_HEADS // 2):
            items.append(_hgrn_item(buf_cur, p, c, fresh, lb, hgain_ref, st_hg, out_ref, 0))
    _sweep(items, fillers[LEAD_CHUNKS:], mall_ref, lvl_ref)
    h_ref[...] = normed(xn_ref)


def _fused_projection_recurrence(x2, norm_w, w_all, wup, bup, ggain, lb_logits, hgain, seq):
    bt = x2.shape[0]
    assert bt % TILE == 0 and seq % TILE == 0
    steps = bt // TILE + 1
    mall = jnp.asarray(_plane_matrix(), dtype=jnp.bfloat16)
    lvl = jnp.asarray(_level_matrix())
    const = lambda a: pl.BlockSpec(a.shape, lambda s: (0,) * a.ndim)
    return pl.pallas_call(
        functools.partial(_fused_kernel, seq // TILE),
        grid=(steps,),
        in_specs=[
            pl.BlockSpec((TILE, D_MODEL), lambda s: (0, 0)),
            pl.BlockSpec((TILE, D_MODEL), lambda s: (jnp.minimum(s + 1, steps - 2), 0)),
            const(norm_w),
            pl.BlockSpec(w_all.shape, lambda s: (0, 0), pipeline_mode=pl.Buffered(1)),
            const(wup), const(bup), const(ggain), const(lb_logits), const(hgain), const(mall), const(lvl),
        ],
        out_specs=pl.BlockSpec((TILE, GLA_WIDTH + HGRN_WIDTH), lambda s: (s, 0)),
        out_shape=jax.ShapeDtypeStruct((steps * TILE, GLA_WIDTH + HGRN_WIDTH), jnp.bfloat16),
        scratch_shapes=[
            pltpu.VMEM((LR_BLK + 1, TILE, LANE), jnp.bfloat16),
            pltpu.VMEM((LR_BLK + 1, TILE, LANE), jnp.bfloat16),
            pltpu.VMEM((TILE, D_MODEL), jnp.bfloat16),
            pltpu.VMEM((GLA_HEADS // 2, 2, GLA_HEAD_V, LANE), jnp.float32),
            pltpu.VMEM((HGRN_HEADS // 2, 2, HGRN_HEAD_V, LANE), jnp.float32),
        ],
        compiler_params=pltpu.CompilerParams(
            dimension_semantics=("arbitrary",),
            vmem_limit_bytes=VMEM_LIMIT),
        name="in_proj_recurrence",
    )(x2, x2, norm_w, w_all, wup, bup, ggain, lb_logits, hgain, mall, lvl)


X_RING = 3


def _out_kernel(m_ref, w_ref, x_hbm, fw_ref, out_ref, wb_ref, xbuf, xsem):
    i = pl.program_id(0)
    tm = out_ref.shape[0]

    def x_copy(step):
        slot = step % X_RING
        return pltpu.make_async_copy(x_hbm.at[pl.ds(pl.multiple_of(step * tm, tm), tm), :], xbuf.at[slot],
                                     xsem.at[slot])

    @pl.when(i == 0)
    def _():
        for step in range(X_RING - 1):
            x_copy(step).start()
        wb_ref[...] = w_ref[0].astype(wb_ref.dtype)

    @pl.when(i + X_RING - 1 < pl.num_programs(0))
    def _():
        x_copy(i + X_RING - 1).start()

    x_copy(i).wait()
    x_ref = xbuf.at[i % X_RING]

    for r0 in range(0, out_ref.shape[0], OUT_ROWS):
        rows = slice(r0, r0 + OUT_ROWS)
        y = jnp.dot(m_ref[rows, :], wb_ref[...], preferred_element_type=jnp.float32) + x_ref[rows, :]
        ms = jnp.mean(y * y, axis=-1, keepdims=True)
        out_ref[rows, :] = y * lax.rsqrt(ms + EPS) * fw_ref[...]


def _output_projection(mixed, row0, wo, x2, final_w, tm=512):
    bt = x2.shape[0]
    d_inner = mixed.shape[1]
    return pl.pallas_call(
        _out_kernel,
        grid=(bt // tm,),
        in_specs=[
            pl.BlockSpec((pl.Element(tm), pl.Element(d_inner)), lambda i: (pl.multiple_of(row0 + i * tm, TILE), 0)),
            pl.BlockSpec((1, d_inner, D_MODEL), lambda i: (0, 0, 0), pipeline_mode=pl.Buffered(1)),
            pl.BlockSpec(memory_space=pl.ANY),
            pl.BlockSpec((1, D_MODEL), lambda i: (0, 0)),
        ],
        out_specs=pl.BlockSpec((tm, D_MODEL), lambda i: (i, 0)),
        out_shape=jax.ShapeDtypeStruct((bt, D_MODEL), jnp.float32),
        scratch_shapes=[pltpu.VMEM((d_inner, D_MODEL), jnp.bfloat16),
                        pltpu.VMEM((X_RING, tm, D_MODEL), jnp.float32),
                        pltpu.SemaphoreType.DMA((X_RING,))],
        compiler_params=pltpu.CompilerParams(
            dimension_semantics=("arbitrary",),
            vmem_limit_bytes=VMEM_LIMIT),
        name="out_proj_residual_norm",
    )(mixed, wo, x2, final_w)


def kernel(x, norm_w, w_in, gla_w_up, gla_b_up, gla_norm_w, hgrn_lb_logits, hgrn_norm_w, w_out, final_norm_w):
    batch, seq, d = x.shape
    assert d == D_MODEL and norm_w.shape[0] == 1
    x2 = x.reshape(batch * seq, d)
    wup = jnp.pad(gla_w_up[0], ((0, LANE - GLA_LOWRANK), (0, 0))).astype(jnp.bfloat16)
    mixed = _fused_projection_recurrence(x2, norm_w, _prepare_in_weight(w_in), wup, gla_b_up, gla_norm_w,
                                         hgrn_lb_logits, hgrn_norm_w, seq)
    out = _output_projection(mixed, TILE, w_out, x2, final_norm_w.reshape(1, d))
    return out.reshape(batch, seq, d)
```
